```python
import math
import jax, jax.numpy as jnp
from jax import lax
import numpy as np

D_MODEL = 2048
BATCH = 2
SEQ = 8192
DEPTH = 1

CTX_LEN = 256
GRID_W = 64
EPS = 1e-6

MLA_HEADS = 8
MLA_Q_RANK = 512
MLA_KV_RANK = 256
MLA_NOPE = 128
MLA_ROPE = 64
MLA_V = 128
MLA_SCALE = (MLA_NOPE + MLA_ROPE) ** -0.5
ROPE_BASE = 10000.0
Q_BLOCK = 128

DN_HEADS = 8
DN_DK = 128
DN_DV = 128
DN_QKV = DN_HEADS * (2 * DN_DK + DN_DV)
CONV_W = 5
CHUNK = 64

PEER_HEADS = 8
PEER_KEYS = 128
PEER_EXPERTS = PEER_KEYS * PEER_KEYS
PEER_QDIM = 256
PEER_TOPK = 16
PEER_BLOCK = 64

IN_SIZES = (MLA_Q_RANK, MLA_KV_RANK, MLA_ROPE, DN_QKV, DN_HEADS * DN_DV,
            DN_HEADS, DN_HEADS, DN_HEADS, DN_HEADS, D_MODEL, D_MODEL)
D_IN = sum(IN_SIZES)
IN_POINTS = tuple(int(p) for p in np.cumsum(IN_SIZES)[:-1])

kernel_name = 'hybrid_mla_gdn_peer_block'


def rmsnorm(x, w):
    xf = x.astype(jnp.float32)
    y = xf * lax.rsqrt(jnp.mean(xf * xf, axis=-1, keepdims=True) + EPS)
    return (y * w.astype(jnp.float32)).astype(x.dtype)


def modulate(x, w, shift, scale):
    return rmsnorm(x, w) * (1 + scale) + shift


def l2norm(x):
    xf = x.astype(jnp.float32)
    return xf * lax.rsqrt(jnp.sum(xf * xf, axis=-1, keepdims=True) + EPS)


def grid_angles(n):
    n_rows = n // GRID_W
    rows = jnp.repeat(jnp.arange(n_rows, dtype=jnp.float32), GRID_W)
    cols = jnp.tile(jnp.arange(GRID_W, dtype=jnp.float32), n_rows)
    axis_dim = MLA_ROPE // 2
    inv_freq = ROPE_BASE ** (-jnp.arange(0, axis_dim, 2, dtype=jnp.float32) / axis_dim)
    return rows[:, None] * inv_freq, cols[:, None] * inv_freq


def rotate(x, ang):
    x1, x2 = jnp.split(x.astype(jnp.float32), 2, axis=-1)
    cos, sin = jnp.cos(ang), jnp.sin(ang)
    return jnp.concatenate([x1 * cos - x2 * sin, x1 * sin + x2 * cos], axis=-1).astype(x.dtype)


def axial_rope(x, ang_r, ang_c):
    xr, xc = jnp.split(x, 2, axis=-1)
    return jnp.concatenate([rotate(xr, ang_r), rotate(xc, ang_c)], axis=-1)


def mla_heads(c_q, c_kv, q_norm_w, kv_norm_w, w_uq, w_ukv):
    b, n = c_q.shape[:2]
    q = (rmsnorm(c_q, q_norm_w) @ w_uq).reshape(b, n, MLA_HEADS, MLA_NOPE + MLA_ROPE)
    kv = (rmsnorm(c_kv, kv_norm_w) @ w_ukv).reshape(b, n, MLA_HEADS, MLA_NOPE + MLA_V)
    return q[..., :MLA_NOPE], q[..., MLA_NOPE:], kv[..., :MLA_NOPE], kv[..., MLA_NOPE:]


def mla_attend(q_nope, q_rope, k_nope, k_rope, v):
    s = (jnp.einsum('bthd,blhd->bhtl', q_nope, k_nope)
         + jnp.einsum('bthr,blr->bhtl', q_rope, k_rope))
    p = jax.nn.softmax(s.astype(jnp.float32) * MLA_SCALE, axis=-1).astype(v.dtype)
    o = jnp.einsum('bhtl,blhd->bthd', p, v)
    return o.reshape(o.shape[0], o.shape[1], MLA_HEADS * MLA_V)


def short_conv(x, w):
    c = x.shape[-1]
    return lax.conv_general_dilated(
        x, w[:, None, :].astype(x.dtype), window_strides=(1,),
        padding=[(CONV_W // 2, CONV_W // 2)],
        dimension_numbers=('NWC', 'WIO', 'NWC'), feature_group_count=c)


def dn_qkv(qkv, conv_w):
    b, n = qkv.shape[:2]
    y = jax.nn.silu(short_conv(qkv, conv_w))
    q, k, v = jnp.split(y, [DN_HEADS * DN_DK, 2 * DN_HEADS * DN_DK], axis=-1)
    q = l2norm(q.reshape(b, n, DN_HEADS, DN_DK)) * DN_DK ** -0.5
    k = l2norm(k.reshape(b, n, DN_HEADS, DN_DK))
    v = v.reshape(b, n, DN_HEADS, DN_DV).astype(jnp.float32)
    return q, k, v


def dn_decay(a, beta_logit, a_log, dt_bias):
    g = -jnp.exp(a_log.astype(jnp.float32)) * jax.nn.softplus(a.astype(jnp.float32) + dt_bias.astype(jnp.float32))
    return g, jax.nn.sigmoid(beta_logit.astype(jnp.float32))


def gated_delta_chunked(q, k, v, g, beta, s0):
    b, n, h, _ = q.shape
    dv = v.shape[-1]
    nc = n // CHUNK

    def chunks(t):
        return t.reshape(b, nc, CHUNK, h, -1).transpose(0, 3, 1, 2, 4)

    q, k, v = chunks(q), chunks(k), chunks(v)
    g = g.reshape(b, nc, CHUNK, h).transpose(0, 3, 1, 2)
    beta = beta.reshape(b, nc, CHUNK, h).transpose(0, 3, 1, 2)
    gc = jnp.cumsum(g, axis=-1)
    incl = jnp.tril(jnp.ones((CHUNK, CHUNK), dtype=bool))
    strict = jnp.tril(jnp.ones((CHUNK, CHUNK), dtype=bool), -1)
    decay = jnp.exp(jnp.where(incl, gc[..., :, None] - gc[..., None, :], -jnp.inf))
    kb = k * beta[..., None]
    m = jnp.where(strict, jnp.einsum('bhnid,bhnjd->bhnij', kb, k) * decay, 0.0)
    rhs = jnp.concatenate([v * beta[..., None], kb * jnp.exp(gc)[..., None]], axis=-1)
    sol = lax.linalg.triangular_solve(m + jnp.eye(CHUNK, dtype=m.dtype), rhs,
                                      left_side=True, lower=True, unit_diagonal=True)
    u, w = sol[..., :dv], sol[..., dv:]
    attn = jnp.where(incl, jnp.einsum('bhnid,bhnjd->bhnij', q, k) * decay, 0.0)
    g_last = gc[..., -1]

    def step(state, xs):
        q_i, k_i, u_i, w_i, a_i, gc_i, gl_i = xs
        v_new = u_i - jnp.einsum('bhck,bhkv->bhcv', w_i, state)
        o_i = (jnp.einsum('bhck,bhkv->bhcv', q_i * jnp.exp(gc_i)[..., None], state)
               + jnp.einsum('bhij,bhjv->bhiv', a_i, v_new))
        k_dec = k_i * jnp.exp(gl_i[..., None] - gc_i)[..., None]
        state = state * jnp.exp(gl_i)[..., None, None] + jnp.einsum('bhck,bhcv->bhkv', k_dec, v_new)
        return state, o_i

    xs = tuple(jnp.moveaxis(t, 2, 0) for t in (q, k, u, w, attn, gc, g_last))
    state, o = lax.scan(step, s0, xs)
    o = o.transpose(1, 0, 3, 2, 4).reshape(b, n, h, dv)
    return o, state


def flip(t):
    return t[:, ::-1]


def dn_gated_out(o, z, norm_w):
    b, n = z.shape[:2]
    o = rmsnorm(o, norm_w) * jax.nn.silu(z.reshape(b, n, DN_HEADS, DN_DV).astype(jnp.float32))
    return o.reshape(b, n, DN_HEADS * DN_DV).astype(z.dtype)


def token_mixer(h, hc, w_in, q_norm_w, kv_norm_w, w_uq, w_ukv, conv_w, a_log, dt_bias,
                dn_norm_w, w_branch_a, w_branch_b, w_out, with_ctx_out):
    b, n, _ = h.shape
    (cq, ckv, kr, qkv, z, a_f, a_b, b_f, b_b, gate_a, gate_b) = jnp.split(h @ w_in, IN_POINTS, axis=-1)
    (cq_c, ckv_c, kr_c, qkv_c, z_c, a_f_c, a_b_c, b_f_c, b_b_c,
     gate_a_c, gate_b_c) = jnp.split(hc @ w_in, IN_POINTS, axis=-1)

    ang_r, ang_c = grid_angles(n)
    qn, qr, kn, v = mla_heads(cq, ckv, q_norm_w, kv_norm_w, w_uq, w_ukv)
    qr = axial_rope(qr, ang_r[:, None, :], ang_c[:, None, :])
    kr = axial_rope(kr, ang_r, ang_c)
    qn_c, qr_c, kn_c, v_c = mla_heads(cq_c, ckv_c, q_norm_w, kv_norm_w, w_uq, w_ukv)
    kn_all = jnp.concatenate([kn_c, kn], axis=1)
    kr_all = jnp.concatenate([kr_c, kr], axis=1)
    v_all = jnp.concatenate([v_c, v], axis=1)
    nb = n // Q_BLOCK

    def to_blocks(t):
        return jnp.swapaxes(t.reshape(b, nb, Q_BLOCK, *t.shape[2:]), 0, 1)

    y_a = lax.map(lambda qs: mla_attend(qs[0], qs[1], kn_all, kr_all, v_all), (to_blocks(qn), to_blocks(qr)))
    y_a = jnp.swapaxes(y_a, 0, 1).reshape(b, n, MLA_HEADS * MLA_V)

    ql, kl, vl = dn_qkv(qkv, conv_w)
    qc, kc, vc = dn_qkv(qkv_c, conv_w)
    g_lf, be_lf = dn_decay(a_f, b_f, a_log[0], dt_bias[0])
    g_lb, be_lb = dn_decay(a_b, b_b, a_log[1], dt_bias[1])
    g_cf, be_cf = dn_decay(a_f_c, b_f_c, a_log[0], dt_bias[0])
    g_cb, be_cb = dn_decay(a_b_c, b_b_c, a_log[1], dt_bias[1])
    s0 = jnp.zeros((b, DN_HEADS, DN_DK, DN_DV), jnp.float32)
    o_cf, s_cf = gated_delta_chunked(qc, kc, vc, g_cf, be_cf, s0)
    o_lf, _ = gated_delta_chunked(ql, kl, vl, g_lf, be_lf, s_cf)
    o_cb, s_cb = gated_delta_chunked(flip(qc), flip(kc), flip(vc), flip(g_cb), flip(be_cb), s0)
    o_lb, _ = gated_delta_chunked(flip(ql), flip(kl), flip(vl), flip(g_lb), flip(be_lb), s_cb)
    y_b = dn_gated_out(o_lf + flip(o_lb), z, dn_norm_w)

    def merge(ya, yb, ga, gb):
        return (jax.nn.sigmoid(ga) * (ya @ w_branch_a) + jax.nn.sigmoid(gb) * (yb @ w_branch_b)) @ w_out

    y = merge(y_a, y_b, gate_a, gate_b)
    if not with_ctx_out:
        return y, None
    y_a_c = mla_attend(qn_c, qr_c, kn_c, kr_c, v_c)
    y_b_c = dn_gated_out(o_cf + flip(o_cb), z_c, dn_norm_w)
    return y, merge(y_a_c, y_b_c, gate_a_c, gate_b_c)


def peer_ffn(h, w_q, sub_keys, u, v):
    b, n, d = h.shape
    nb = (b * n) // PEER_BLOCK
    hb = h.reshape(nb, PEER_BLOCK, d)

    def block(hx):
        t = hx.shape[0]
        q = (hx @ w_q).reshape(t, PEER_HEADS, 2, PEER_QDIM // 2).astype(jnp.float32)
        s = jnp.einsum('thpd,pkd->thpk', q, sub_keys.astype(jnp.float32))
        top_s, top_i = lax.top_k(s, PEER_TOPK)
        cand = top_s[:, :, 0, :, None] + top_s[:, :, 1, None, :]
        cand_idx = top_i[:, :, 0, :, None] * PEER_KEYS + top_i[:, :, 1, None, :]
        best_s, best_pos = lax.top_k(cand.reshape(t, PEER_HEADS, -1), PEER_TOPK)
        idx = jnp.take_along_axis(cand_idx.reshape(t, PEER_HEADS, -1), best_pos, axis=-1)
        wts = jax.nn.softmax(best_s, axis=-1).astype(hx.dtype)
        act = jax.nn.gelu(jnp.einsum('td,thkd->thk', hx, u[idx]), approximate=False)
        return jnp.einsum('thk,thkd->td', wts * act, v[idx])

    return lax.map(block, hb).reshape(b, n, d)


def setup_inputs(seed: int = 0) -> dict:
    key = jax.random.key(seed)
    ks = jax.random.split(key, 26)
    f32 = jnp.float32
    L, D = DEPTH, D_MODEL

    def nrm(k, shape, scale=1.0):
        return jax.random.normal(k, shape, f32) * scale

    dt = jnp.exp(jax.random.uniform(ks[14], (L, 2, DN_HEADS), f32, math.log(1e-3), math.log(1e-1)))
    return {
        'x': nrm(ks[0], (BATCH, SEQ, D)),
        'c': nrm(ks[1], (BATCH, D)),
        'ctx': nrm(ks[2], (BATCH, CTX_LEN, D)),
        'c_ctx': nrm(ks[3], (D,)),
        'w_mod': nrm(ks[4], (L, D, 6 * D), 0.5 * D ** -0.5),
        'b_mod': nrm(ks[5], (L, 6 * D), 0.02),
        'norm1_w': 1.0 + nrm(ks[6], (L, D), 0.1),
        'w_in': nrm(ks[7], (L, D, D_IN), D ** -0.5),
        'mla_q_norm_w': 1.0 + nrm(ks[8], (L, MLA_Q_RANK), 0.1),
        'mla_kv_norm_w': 1.0 + nrm(ks[9], (L, MLA_KV_RANK), 0.1),
        'w_uq': nrm(ks[10], (L, MLA_Q_RANK, MLA_HEADS * (MLA_NOPE + MLA_ROPE)), MLA_Q_RANK ** -0.5),
        'w_ukv': nrm(ks[11], (L, MLA_KV_RANK, MLA_HEADS * (MLA_NOPE + MLA_V)), MLA_KV_RANK ** -0.5),
        'dn_conv_w': nrm(ks[12], (L, CONV_W, DN_QKV), CONV_W ** -0.5),
        'dn_a_log': jnp.log(jax.random.uniform(ks[13], (L, 2, DN_HEADS), f32, 1.0, 16.0)),
        'dn_dt_bias': dt + jnp.log(-jnp.expm1(-dt)),
        'dn_norm_w': 1.0 + nrm(ks[15], (L, DN_DV), 0.1),
        'w_branch_a': nrm(ks[16], (L, MLA_HEADS * MLA_V, D), (MLA_HEADS * MLA_V) ** -0.5),
        'w_branch_b': nrm(ks[17], (L, DN_HEADS * DN_DV, D), (DN_HEADS * DN_DV) ** -0.5),
        'w_out': nrm(ks[18], (L, D, D), D ** -0.5),
        'norm2_w': 1.0 + nrm(ks[19], (L, D), 0.1),
        'peer_w_q': nrm(ks[20], (L, D, PEER_HEADS * PEER_QDIM), D ** -0.5),
        'peer_sub_keys': nrm(ks[21], (L, 2, PEER_KEYS, PEER_QDIM // 2), (PEER_QDIM // 2) ** -0.5),
        'peer_u': nrm(ks[22], (L, PEER_EXPERTS, D), D ** -0.5),
        'peer_v': nrm(ks[23], (L, PEER_EXPERTS, D), PEER_HEADS ** -0.5),
        'final_norm_w': 1.0 + nrm(ks[24], (D,), 0.1),
    }


def reference(x, c, ctx, c_ctx, w_mod, b_mod, norm1_w, w_in, mla_q_norm_w, mla_kv_norm_w,
              w_uq, w_ukv, dn_conv_w, dn_a_log, dn_dt_bias, dn_norm_w, w_branch_a, w_branch_b,
              w_out, norm2_w, peer_w_q, peer_sub_keys, peer_u, peer_v, final_norm_w):
    silu_c = jax.nn.silu(c)
    silu_cc = jax.nn.silu(c_ctx)
    for i in range(DEPTH):
        last = i == DEPTH - 1
        mod = (silu_c @ w_mod[i] + b_mod[i])[:, None, :]
        sh1, sc1, g1, sh2, sc2, g2 = jnp.split(mod, 6, axis=-1)
        mod_c = silu_cc @ w_mod[i] + b_mod[i]
        shc1, scc1, gc1, shc2, scc2, gc2 = jnp.split(mod_c, 6, axis=-1)
        h = modulate(x, norm1_w[i], sh1, sc1)
        hc = modulate(ctx, norm1_w[i], shc1, scc1)
        y, yc = token_mixer(h, hc, w_in[i], mla_q_norm_w[i], mla_kv_norm_w[i], w_uq[i], w_ukv[i],
                            dn_conv_w[i], dn_a_log[i], dn_dt_bias[i], dn_norm_w[i],
                            w_branch_a[i], w_branch_b[i], w_out[i], not last)
        x = x + g1 * y
        x = x + g2 * peer_ffn(modulate(x, norm2_w[i], sh2, sc2), peer_w_q[i], peer_sub_keys[i], peer_u[i], peer_v[i])
        if not last:
            ctx = ctx + gc1 * yc
            ctx = ctx + gc2 * peer_ffn(modulate(ctx, norm2_w[i], shc2, scc2), peer_w_q[i], peer_sub_keys[i], peer_u[i], peer_v[i])
    return rmsnorm(x, final_norm_w)
```

```python
import functools
import math

import jax
import jax.numpy as jnp
import numpy as np
from jax import lax
from jax.experimental import pallas as pl
from jax.experimental.pallas import tpu as pltpu

F32 = jnp.float32
BF16 = jnp.bfloat16
HIGHEST = lax.Precision.HIGHEST

D_MODEL = 2048
EPS = 1e-6
GRID_W = 64
ROPE_BASE = 10000.0
HEADS = 8
MLA_Q_RANK = 512
MLA_KV_RANK = 256
MLA_NOPE = 128
MLA_ROPE = 64
MLA_V = 128
MLA_SCALE = (MLA_NOPE + MLA_ROPE) ** -0.5
QK_PAD = 256
DN_DK = 128
DN_DV = 128
DN_QKV = HEADS * (2 * DN_DK + DN_DV)
CONV_W = 5
CHUNK = 64
PEER_KEYS = 128
PEER_EXPERTS = PEER_KEYS * PEER_KEYS
PEER_TOPK = 16
D_IN_PAD = 9216

COL_MISC = 0
COL_Z = 1
COL_GA = 1
COL_GB = 2
COL_QKV = 6
MISC_GATES = 896

VMEM_LIMIT = 56 * 1024 * 1024


def _cparams(*sem):
    return pltpu.CompilerParams(dimension_semantics=sem, vmem_limit_bytes=VMEM_LIMIT)


def _rms(x):
    return x * lax.rsqrt(jnp.mean(x * x, axis=-1, keepdims=True) + EPS)


def _silu(x):
    return x * jax.nn.sigmoid(x)


def _dot(a, b):
    return jnp.dot(a, b, preferred_element_type=F32)


def _dot_nt(a, b, precision=None):
    return lax.dot_general(a, b, (((1,), (1,)), ((), ())), precision=precision,
                           preferred_element_type=F32)


def _dot_tn(a, b):
    return lax.dot_general(a, b, (((0,), (0,)), ((), ())), preferred_element_type=F32)


def _mod_kernel(c_ref, w_ref, b_ref, o_ref):
    s = _silu(c_ref[...])
    o_ref[...] = jnp.dot(s, w_ref[...], precision=HIGHEST, preferred_element_type=F32) + b_ref[...]


def _mod(cmat, w_mod, b_mod):
    n_out = w_mod.shape[1]
    tn = 1536
    return pl.pallas_call(
        _mod_kernel,
        grid=(n_out // tn,),
        in_specs=[pl.BlockSpec((8, D_MODEL), lambda j: (0, 0)),
                  pl.BlockSpec((D_MODEL, tn), lambda j: (0, j)),
                  pl.BlockSpec((1, tn), lambda j: (0, j))],
        out_specs=pl.BlockSpec((8, tn), lambda j: (0, j)),
        out_shape=jax.ShapeDtypeStruct((8, n_out), F32),
        compiler_params=_cparams("arbitrary"),
        name="mod",
    )(cmat, w_mod, b_mod)


def _in_proj_kernel(x_ref, sh_ref, sc_ref, nw_ref, w_ref, o_ref, h_scr, *, tiles_per_batch, row0):
    @pl.when(pl.program_id(1) == 0)
    def _():
        r = row0 + pl.program_id(0) // tiles_per_batch
        sh = sh_ref[pl.ds(r, 1), :]
        sc = sc_ref[pl.ds(r, 1), :]
        y = _rms(x_ref[...]) * nw_ref[...]
        h_scr[...] = (y * (1.0 + sc) + sh).astype(BF16)

    o_ref[...] = _dot(h_scr[...], w_ref[...])


def _in_proj(x2d, mod, norm_w, w_perm, *, rows_per_batch, row0):
    m = x2d.shape[0]
    tm = min(1024, rows_per_batch)
    tn = 1024
    kern = functools.partial(_in_proj_kernel, tiles_per_batch=rows_per_batch // tm, row0=row0)
    return pl.pallas_call(
        kern,
        grid=(m // tm, D_IN_PAD // tn),
        in_specs=[pl.BlockSpec((tm, D_MODEL), lambda i, j: (i, 0)),
                  pl.BlockSpec((8, D_MODEL), lambda i, j: (0, 0)),
                  pl.BlockSpec((8, D_MODEL), lambda i, j: (0, 1)),
                  pl.BlockSpec((1, D_MODEL), lambda i, j: (0, 0)),
                  pl.BlockSpec((D_MODEL, tn), lambda i, j: (0, j))],
        out_specs=pl.BlockSpec((tm, tn), lambda i, j: (i, j)),
        out_shape=jax.ShapeDtypeStruct((m, D_IN_PAD), F32),
        scratch_shapes=[pltpu.VMEM((tm, D_MODEL), BF16)],
        compiler_params=_cparams("arbitrary", "arbitrary"),
        name="in_proj",
    )(x2d, mod, mod, norm_w, w_perm)


def _mla_prep_kernel(p_ref, qnw_ref, kvnw_ref, wuq_ref, wukv_ref, cq_ref, sq_ref, *out_refs, with_q):
    if with_q:
        q_ref, k_ref, v_ref = out_refs
    else:
        k_ref, v_ref = out_refs
    p = p_ref[...]
    tm = p.shape[0]
    zpad = jnp.zeros((tm, QK_PAD - MLA_NOPE - MLA_ROPE), F32)
    ckv = p[:, MLA_Q_RANK:MLA_Q_RANK + MLA_KV_RANK]
    kr = p[:, 768:832]
    kr_sw = p[:, 832:896]
    cosq = cq_ref[...]
    sinq = sq_ref[...]
    kv = _dot((_rms(ckv) * kvnw_ref[...]).astype(BF16), wukv_ref[...])
    kr_rot = kr * cosq[:, :MLA_ROPE] + kr_sw * sinq[:, :MLA_ROPE]
    for h in range(HEADS):
        kh = jnp.concatenate([kv[:, h * MLA_NOPE:(h + 1) * MLA_NOPE], kr_rot, zpad], axis=-1)
        k_ref[h] = kh.astype(BF16)
        v_ref[h] = kv[:, HEADS * MLA_NOPE + h * MLA_V:HEADS * MLA_NOPE + (h + 1) * MLA_V].astype(BF16)
    if with_q:
        cqv = p[:, :MLA_Q_RANK]
        q = _dot((_rms(cqv) * qnw_ref[...]).astype(BF16), wuq_ref[...])
        n0 = HEADS * MLA_NOPE
        n1 = n0 + HEADS * MLA_ROPE
        qr_rot = q[:, n0:n1] * cosq + q[:, n1:] * sinq
        for h in range(HEADS):
            qh = jnp.concatenate([q[:, h * MLA_NOPE:(h + 1) * MLA_NOPE],
                                  qr_rot[:, h * MLA_ROPE:(h + 1) * MLA_ROPE], zpad], axis=-1)
            q_ref[h] = (qh * MLA_SCALE).astype(BF16)


def _mla_prep(p3, qnw, kvnw, wuq, wukv, cos_t, sin_t, *, with_q):
    b, n, _ = p3.shape
    tm = min(256, n)
    qk_spec = pl.BlockSpec((None, HEADS, tm, QK_PAD), lambda bi, i: (bi, 0, i, 0))
    v_spec = pl.BlockSpec((None, HEADS, tm, MLA_V), lambda bi, i: (bi, 0, i, 0))
    qk_shape = jax.ShapeDtypeStruct((b, HEADS, n, QK_PAD), BF16)
    v_shape = jax.ShapeDtypeStruct((b, HEADS, n, MLA_V), BF16)
    out_specs = [qk_spec, v_spec]
    out_shape = [qk_shape, v_shape]
    if with_q:
        out_specs = [qk_spec] + out_specs
        out_shape = [qk_shape] + out_shape
    full = lambda shape: pl.BlockSpec(shape, lambda bi, i: (0,) * len(shape))
    return pl.pallas_call(
        functools.partial(_mla_prep_kernel, with_q=with_q),
        grid=(b, n // tm),
        in_specs=[pl.BlockSpec((None, tm, 1024), lambda bi, i: (bi, i, COL_MISC)),
                  full((1, MLA_Q_RANK)), full((1, MLA_KV_RANK)),
                  full(wuq.shape), full(wukv.shape),
                  pl.BlockSpec((tm, HEADS * MLA_ROPE), lambda bi, i: (i, 0)),
                  pl.BlockSpec((tm, HEADS * MLA_ROPE), lambda bi, i: (i, 0))],
        out_specs=out_specs,
        out_shape=out_shape,
        compiler_params=_cparams("arbitrary", "arbitrary"),
        name="mla_prep_q" if with_q else "mla_prep_kv",
    )(p3, qnw, kvnw, wuq, wukv, cos_t, sin_t)


def _flash_kernel(q_ref, kc_ref, vc_ref, kl_ref, vl_ref, o_ref, *, tk, n_chunks):
    q = q_ref[...]
    tq = q.shape[0]

    def step(k, v, carry):
        m, l, acc = carry
        s = _dot_nt(q, k)
        m_new = jnp.maximum(m, jnp.max(s, axis=-1, keepdims=True))
        p = jnp.exp(s - m_new)
        alpha = jnp.exp(m - m_new)
        l = alpha * l + jnp.sum(p, axis=-1, keepdims=True)
        acc = alpha * acc + _dot(p.astype(BF16), v)
        return m_new, l, acc

    init = (jnp.full((tq, 1), -jnp.inf, F32), jnp.zeros((tq, 1), F32), jnp.zeros((tq, MLA_V), F32))
    carry = step(kc_ref[...], vc_ref[...], init)

    def body(c, carry):
        off = pl.multiple_of(c * tk, tk)
        return step(kl_ref[pl.ds(off, tk), :], vl_ref[pl.ds(off, tk), :], carry)

    _, l, acc = lax.fori_loop(0, n_chunks, body, carry)
    o_ref[...] = (acc / l).astype(o_ref.dtype)


def _flash(q, kc, vc, kl, vl):
    b, h, n, _ = q.shape
    nc = kc.shape[2]
    tq = min(256, n)
    tk = min(512, n)
    return pl.pallas_call(
        functools.partial(_flash_kernel, tk=tk, n_chunks=n // tk),
        grid=(b, h, n // tq),
        in_specs=[pl.BlockSpec((None, None, tq, QK_PAD), lambda bi, hi, i: (bi, hi, i, 0)),
                  pl.BlockSpec((None, None, nc, QK_PAD), lambda bi, hi, i: (bi, hi, 0, 0)),
                  pl.BlockSpec((None, None, nc, MLA_V), lambda bi, hi, i: (bi, hi, 0, 0)),
                  pl.BlockSpec((None, None, n, QK_PAD), lambda bi, hi, i: (bi, hi, 0, 0)),
                  pl.BlockSpec((None, None, n, MLA_V), lambda bi, hi, i: (bi, hi, 0, 0))],
        out_specs=pl.BlockSpec((None, tq, MLA_V), lambda bi, hi, i: (bi, i, hi)),
        out_shape=jax.ShapeDtypeStruct((b, n, h * MLA_V), BF16),
        compiler_params=_cparams("arbitrary", "arbitrary", "arbitrary"),
        name="flash",
    )(q, kc, vc, kl, vl)


def _dn_conv_kernel(cur_ref, prev_ref, next_ref, cw_ref, gl_ref, alog_ref, dtb_ref, o_ref, g_ref, scr, *, n_tiles):
    i = pl.program_id(1)
    part = pl.program_id(2)
    tm = cur_ref.shape[0]
    scr[pl.ds(0, 8), :] = jnp.where(i > 0, prev_ref[...], 0.0)
    scr[pl.ds(8, tm), :] = cur_ref[...]
    scr[pl.ds(8 + tm, 8), :] = jnp.where(i < n_tiles - 1, next_ref[...], 0.0)
    cw = cw_ref[...]
    y = scr[pl.ds(8 - CONV_W // 2, tm), :] * cw[0:1, :]
    for w in range(1, CONV_W):
        y = y + scr[pl.ds(8 - CONV_W // 2 + w, tm), :] * cw[w:w + 1, :]
    y = _silu(y)

    @pl.when(part == 2)
    def _():
        o_ref[...] = y

    @pl.when(part < 2)
    def _():
        scale = jnp.where(part == 0, DN_DK ** -0.5, 1.0).astype(F32)
        for h in range(HEADS):
            yh = y[:, h * DN_DK:(h + 1) * DN_DK]
            yh = yh * lax.rsqrt(jnp.sum(yh * yh, axis=-1, keepdims=True) + EPS)
            o_ref[:, h * DN_DK:(h + 1) * DN_DK] = yh * scale

    @pl.when(part == 0)
    def _():
        val = gl_ref[...]
        lane = lax.broadcasted_iota(jnp.int32, val.shape, 1)
        g = -jnp.exp(alog_ref[...]) * jax.nn.softplus(val + dtb_ref[...])
        g_ref[...] = jnp.where(lane < 2 * HEADS, g, jax.nn.sigmoid(val))


def _dn_conv(p3, conv_w, alog_row, dtb_row):
    b, n, _ = p3.shape
    tm = min(256, n)
    nt = n // tm
    r8 = tm // 8
    return pl.pallas_call(
        functools.partial(_dn_conv_kernel, n_tiles=nt),
        grid=(b, nt, 3),
        in_specs=[pl.BlockSpec((None, tm, 1024), lambda bi, i, p: (bi, i, COL_QKV + p)),
                  pl.BlockSpec((None, 8, 1024), lambda bi, i, p: (bi, jnp.maximum(i * r8 - 1, 0), COL_QKV + p)),
                  pl.BlockSpec((None, 8, 1024), lambda bi, i, p: (bi, jnp.minimum((i + 1) * r8, nt * r8 - 1), COL_QKV + p)),
                  pl.BlockSpec((CONV_W, 1024), lambda bi, i, p: (0, p)),
                  pl.BlockSpec((None, tm, 128), lambda bi, i, p: (bi, i, MISC_GATES // 128)),
                  pl.BlockSpec((1, 128), lambda bi, i, p: (0, 0)),
                  pl.BlockSpec((1, 128), lambda bi, i, p: (0, 0))],
        out_specs=[pl.BlockSpec((None, tm, 1024), lambda bi, i, p: (bi, i, p)),
                   pl.BlockSpec((None, tm, 128), lambda bi, i, p: (bi, i, 0))],
        out_shape=[jax.ShapeDtypeStruct((b, n, DN_QKV), F32),
                   jax.ShapeDtypeStruct((b, n, 128), F32)],
        scratch_shapes=[pltpu.VMEM((tm + 16, 1024), F32)],
        compiler_params=_cparams("arbitrary", "arbitrary", "arbitrary"),
        name="dn_conv",
    )(p3, p3, p3, conv_w, p3, alog_row, dtb_row)


def _dn_chunk_kernel(q_ref, k_ref, v_ref, g_ref, *out_refs, chunks_per_step):
    ri = lax.broadcasted_iota(jnp.int32, (CHUNK, CHUNK), 0)
    ci = lax.broadcasted_iota(jnp.int32, (CHUNK, CHUNK), 1)
    eye = (ri == ci).astype(F32)
    incl = (ri >= ci, ri <= ci)
    strict = (ri > ci, ri < ci)
    cum = (incl[0].astype(F32), incl[1].astype(F32))

    def chunk_body(c, _):
        r0 = pl.multiple_of(c * CHUNK, CHUNK)
        rows = pl.ds(r0, CHUNK)
        gates = g_ref[rows, :]
        for d in range(2):
            u_ref, w_ref, qg_ref, kd_ref, att_ref, egl_ref = out_refs[6 * d:6 * d + 6]
            gc = jnp.dot(cum[d], gates, precision=HIGHEST, preferred_element_type=F32)
            gct = gc.T
            last = CHUNK - 1 if d == 0 else 0
            for h in range(HEADS):
                cs = slice(h * DN_DK, (h + 1) * DN_DK)
                q = q_ref[rows, cs]
                k = k_ref[rows, cs]
                v = v_ref[rows, cs]
                kb16 = k.astype(BF16)
                kk = _dot_nt(kb16, kb16)
                qk = _dot_nt(q.astype(BF16), kb16)
                col = d * HEADS + h
                gcol = gc[:, col:col + 1]
                grow = gct[col:col + 1, :]
                beta = gates[:, 2 * HEADS + col:2 * HEADS + col + 1]
                glast = gc[last:last + 1, col:col + 1]
                decay = jnp.exp(jnp.where(incl[d], gcol - grow, -jnp.inf))
                x = jnp.where(strict[d], -(beta * kk * decay), 0.0)
                t = eye + x
                pw = x
                for _ in range(5):
                    pw16 = pw.astype(BF16)
                    pw = _dot(pw16, pw16)
                    t = t + _dot(t.astype(BF16), pw.astype(BF16))
                kbeta = k * beta
                rhs = jnp.concatenate([v * beta, kbeta * jnp.exp(gcol)], axis=-1).astype(BF16)
                sol = _dot(t.astype(BF16), rhs)
                u_ref[rows, cs] = sol[:, :DN_DV]
                w_ref[rows, cs] = sol[:, DN_DV:].astype(BF16)
                qg_ref[rows, cs] = (q * jnp.exp(gcol)).astype(BF16)
                kd_ref[rows, cs] = (k * jnp.exp(glast - gcol)).astype(BF16)
                att_ref[rows, h * CHUNK:(h + 1) * CHUNK] = jnp.where(incl[d], qk * decay, 0.0).astype(BF16)
                egl_ref[pl.ds(c * HEADS + h, 1), :] = jnp.broadcast_to(jnp.exp(glast), (1, 128))
        return 0

    lax.fori_loop(0, chunks_per_step, chunk_body, 0)


def _dn_chunk(qkvn, gates):
    b, n, _ = qkvn.shape
    nchunks = n // CHUNK
    cb = min(4, nchunks)
    tm = cb * CHUNK
    hd = HEADS * DN_DK
    row = lambda w: pl.BlockSpec((None, tm, w), lambda bi, i: (bi, i, 0))
    one_dir_specs = [row(hd), row(hd), row(hd), row(hd), row(HEADS * CHUNK),
                     pl.BlockSpec((None, cb * HEADS, 128), lambda bi, i: (bi, i, 0))]
    one_dir_shapes = [jax.ShapeDtypeStruct((b, n, hd), F32),
                      jax.ShapeDtypeStruct((b, n, hd), BF16),
                      jax.ShapeDtypeStruct((b, n, hd), BF16),
                      jax.ShapeDtypeStruct((b, n, hd), BF16),
                      jax.ShapeDtypeStruct((b, n, HEADS * CHUNK), BF16),
                      jax.ShapeDtypeStruct((b, nchunks * HEADS, 128), F32)]
    outs = pl.pallas_call(
        functools.partial(_dn_chunk_kernel, chunks_per_step=cb),
        grid=(b, nchunks // cb),
        in_specs=[pl.BlockSpec((None, tm, hd), lambda bi, i: (bi, i, 0)),
                  pl.BlockSpec((None, tm, hd), lambda bi, i: (bi, i, 1)),
                  pl.BlockSpec((None, tm, hd), lambda bi, i: (bi, i, 2)),
                  pl.BlockSpec((None, tm, 128), lambda bi, i: (bi, i, 0))],
        out_specs=one_dir_specs * 2,
        out_shape=one_dir_shapes * 2,
        compiler_params=_cparams("arbitrary", "arbitrary"),
        name="dn_chunk",
    )(qkvn, qkvn, qkvn, gates)
    return outs[:6], outs[6:]


def _dn_scan_kernel(s0_ref, *refs, n_batch):
    ins = refs[:12]
    of_ref, ob_ref, sf_ref, s_scr = refs[12:]
    step = pl.program_id(0)

    @pl.when(step == 0)
    def _():
        s_scr[...] = s0_ref[...]

    for d in range(2):
        u_ref, w_ref, qg_ref, kd_ref, att_ref, egl_ref = ins[6 * d:6 * d + 6]
        o_ref = of_ref if d == 0 else ob_ref

        def batch_body(b, _):
            for h in range(HEADS):
                cs = slice(h * DN_DK, (h + 1) * DN_DK)
                s = s_scr[d, b, h]
                s16 = s.astype(BF16)
                v_new = u_ref[b, :, cs] - _dot(w_ref[b, :, cs], s16)
                v16 = v_new.astype(BF16)
                o_ref[b, :, cs] = _dot(qg_ref[b, :, cs], s16) + _dot(att_ref[b, :, h * CHUNK:(h + 1) * CHUNK], v16)
                s_scr[d, b, h] = s * egl_ref[b, h:h + 1, :] + _dot_tn(kd_ref[b, :, cs], v16)
            return 0

        lax.fori_loop(0, n_batch, batch_body, 0)

    @pl.when(step == pl.num_programs(0) - 1)
    def _():
        sf_ref[...] = s_scr[...]


def _dn_scan(s0, fwd, bwd):
    b, n, hd = fwd[0].shape
    nchunks = n // CHUNK
    f_idx = lambda s: (0, s, 0)
    b_idx = lambda s: (0, nchunks - 1 - s, 0)

    def specs(idx):
        return [pl.BlockSpec((b, CHUNK, hd), idx)] * 4 + [pl.BlockSpec((b, CHUNK, HEADS * CHUNK), idx),
                                                          pl.BlockSpec((b, HEADS, 128), idx)]

    state_spec = pl.BlockSpec(s0.shape, lambda s: (0,) * 5)
    o_f, o_b, s_fin = pl.pallas_call(
        functools.partial(_dn_scan_kernel, n_batch=b),
        grid=(nchunks,),
        in_specs=[state_spec] + specs(f_idx) + specs(b_idx),
        out_specs=[pl.BlockSpec((b, CHUNK, hd), f_idx), pl.BlockSpec((b, CHUNK, hd), b_idx), state_spec],
        out_shape=[jax.ShapeDtypeStruct((b, n, hd), F32), jax.ShapeDtypeStruct((b, n, hd), F32),
                   jax.ShapeDtypeStruct(s0.shape, F32)],
        scratch_shapes=[pltpu.VMEM(s0.shape, F32)],
        compiler_params=_cparams("arbitrary"),
        name="dn_scan",
    )(s0, *fwd, *bwd)
    return o_f, o_b, s_fin


def _merge_kernel(ya_ref, of_ref, ob_ref, z_ref, ga_ref, gb_ref, x_ref, g1_ref, sh2_ref, sc2_ref,
                  dnw_ref, n2w_ref, wa_ref, wb_ref, wo_ref, x1_ref, h2_ref):
    b = pl.program_id(0)
    o = of_ref[...] + ob_ref[...]
    z = z_ref[...]
    dnw = dnw_ref[...]
    parts = []
    for h in range(HEADS):
        cs = slice(h * DN_DV, (h + 1) * DN_DV)
        parts.append(_rms(o[:, cs]) * dnw * _silu(z[:, cs]))
    yb = jnp.concatenate(parts, axis=-1).astype(BF16)
    t = (jax.nn.sigmoid(ga_ref[...]) * _dot(ya_ref[...], wa_ref[...])
         + jax.nn.sigmoid(gb_ref[...]) * _dot(yb, wb_ref[...]))
    y = _dot(t.astype(BF16), wo_ref[...])
    x1 = x_ref[...] + g1_ref[pl.ds(b, 1), :] * y
    x1_ref[...] = x1
    h2 = _rms(x1) * n2w_ref[...] * (1.0 + sc2_ref[pl.ds(b, 1), :]) + sh2_ref[pl.ds(b, 1), :]
    h2_ref[...] = h2.astype(BF16)


def _merge(ya, o_f, o_b, p3, x, mod, dn_norm_w, norm2_w, wa, wb, wo):
    b, n, _ = x.shape
    tm = min(256, n)
    row = lambda w, j=0: pl.BlockSpec((None, tm, w), lambda bi, i: (bi, i, j))
    modrow = lambda j: pl.BlockSpec((8, D_MODEL), lambda bi, i: (0, j))
    const = lambda shape: pl.BlockSpec(shape, lambda bi, i: (0,) * len(shape), pipeline_mode=pl.Buffered(1))
    return pl.pallas_call(
        _merge_kernel,
        grid=(b, n // tm),
        in_specs=[row(1024), row(1024), row(1024), row(1024, COL_Z), row(D_MODEL, COL_GA), row(D_MODEL, COL_GB),
                  row(D_MODEL), modrow(2), modrow(3), modrow(4),
                  const((1, DN_DV)), const((1, D_MODEL)),
                  const(wa.shape), const(wb.shape), const(wo.shape)],
        out_specs=[row(D_MODEL), row(D_MODEL)],
        out_shape=[jax.ShapeDtypeStruct((b, n, D_MODEL), F32), jax.ShapeDtypeStruct((b, n, D_MODEL), BF16)],
        compiler_params=_cparams("arbitrary", "arbitrary"),
        name="merge",
    )(ya, o_f, o_b, p3, p3, p3, x, mod, mod, mod, dn_norm_w, norm2_w, wa, wb, wo)


def _top16_rows(w):
    vals = []
    for _ in range(PEER_TOPK):
        m = jnp.max(w, axis=0, keepdims=True)
        vals.append(m)
        w = jnp.where(w == m, -jnp.inf, w)
    return vals


def _peer_topk_kernel(h2_ref, wq_ref, sk_ref, s0_ref, s1_ref, e0_ref, e1_ref, tau_ref, q_scr):
    q_scr[...] = _dot(h2_ref[...], wq_ref[...])
    pairs = [(m, n) for m in range(PEER_TOPK) for n in range(PEER_TOPK) if (m + 1) * (n + 1) <= PEER_TOPK]

    def head_body(h, _):
        off0 = pl.multiple_of(h * 2 * PEER_KEYS, PEER_KEYS)
        off1 = pl.multiple_of(h * 2 * PEER_KEYS + PEER_KEYS, PEER_KEYS)
        s0 = _dot_nt(sk_ref[0], q_scr[:, pl.ds(off0, PEER_KEYS)], precision=HIGHEST)
        s1 = _dot_nt(sk_ref[1], q_scr[:, pl.ds(off1, PEER_KEYS)], precision=HIGHEST)
        a = _top16_rows(s0)
        bb = _top16_rows(s1)
        cands = [a[m] + bb[n] for m, n in pairs]
        best = []
        for _ in range(PEER_TOPK):
            m = functools.reduce(jnp.maximum, cands)
            best.append(m)
            cands = [jnp.where(cv == m, -jnp.inf, cv) for cv in cands]
        zsum = functools.reduce(lambda p, r: p + r, [jnp.exp(bv - best[0]) for bv in best])
        s0_ref[h] = s0
        s1_ref[h] = s1
        e0_ref[h] = jnp.exp(s0 - a[0]) / zsum
        e1_ref[h] = jnp.exp(s1 - bb[0])
        tau_ref[pl.ds(h, 1), :] = best[PEER_TOPK - 1]
        return 0

    lax.fori_loop(0, HEADS, head_body, 0)


def _peer_topk(h2, wq, sub_keys):
    t = h2.shape[0]
    tn = min(256, t)
    big = pl.BlockSpec((HEADS, PEER_KEYS, tn), lambda i: (0, 0, i))
    big_shape = jax.ShapeDtypeStruct((HEADS, PEER_KEYS, t), F32)
    return pl.pallas_call(
        _peer_topk_kernel,
        grid=(t // tn,),
        in_specs=[pl.BlockSpec((tn, D_MODEL), lambda i: (i, 0)),
                  pl.BlockSpec(wq.shape, lambda i: (0, 0), pipeline_mode=pl.Buffered(1)),
                  pl.BlockSpec(sub_keys.shape, lambda i: (0, 0, 0))],
        out_specs=[big, big, big, big, pl.BlockSpec((HEADS, tn), lambda i: (0, i))],
        out_shape=[big_shape, big_shape, big_shape, big_shape, jax.ShapeDtypeStruct((HEADS, t), F32)],
        scratch_shapes=[pltpu.VMEM((tn, wq.shape[1]), F32)],
        compiler_params=_cparams("arbitrary"),
        name="peer_topk",
    )(h2, wq, sub_keys)


def _peer_dense_kernel(h2_ref, u_ref, v_ref, s0_ref, s1_ref, e0_ref, e1_ref, tau_ref, x1_ref, g2_ref, fw_ref,
                       o_ref, acc, wsum, *, tiles_per_batch):
    e = pl.program_id(1)
    eb = u_ref.shape[0]
    tn = h2_ref.shape[0]

    @pl.when(e == 0)
    def _():
        acc[...] = jnp.zeros_like(acc)

    a = _dot_nt(u_ref[...], h2_ref[...])
    for il in range(eb // PEER_KEYS):
        for lg in range(tn // 128):
            ls = slice(lg * 128, (lg + 1) * 128)
            tot = jnp.zeros((PEER_KEYS, 128), F32)
            for h in range(HEADS):
                s0row = s0_ref[h, il:il + 1, ls]
                e0row = e0_ref[h, il:il + 1, ls]
                tau = tau_ref[h:h + 1, ls]
                sel = (s0row + s1_ref[h, :, ls]) >= tau
                tot = tot + jnp.where(sel, e0row * e1_ref[h, :, ls], 0.0)
            wsum[il * PEER_KEYS:(il + 1) * PEER_KEYS, ls] = tot
    act = 0.5 * a * (1.0 + lax.erf(a * (2.0 ** -0.5)))
    g = (act * wsum[...]).astype(BF16)
    acc[...] += _dot_tn(g, v_ref[...])

    @pl.when(e == pl.num_programs(1) - 1)
    def _():
        b = pl.program_id(0) // tiles_per_batch
        xo = x1_ref[...] + g2_ref[pl.ds(b, 1), :] * acc[...]
        o_ref[...] = _rms(xo) * fw_ref[...]


def _peer_dense(h2, u16, v16, s0, s1, e0, e1, tau, x1, mod, final_w, *, rows_per_batch):
    t = h2.shape[0]
    tn = min(512, rows_per_batch)
    eb = 8 * PEER_KEYS
    big = pl.BlockSpec((HEADS, PEER_KEYS, tn), lambda i, e: (0, 0, i))
    rows = pl.BlockSpec((HEADS, eb // PEER_KEYS, tn), lambda i, e: (0, e, i))
    return pl.pallas_call(
        functools.partial(_peer_dense_kernel, tiles_per_batch=rows_per_batch // tn),
        grid=(t // tn, PEER_EXPERTS // eb),
        in_specs=[pl.BlockSpec((tn, D_MODEL), lambda i, e: (i, 0)),
                  pl.BlockSpec((eb, D_MODEL), lambda i, e: (e, 0)),
                  pl.BlockSpec((eb, D_MODEL), lambda i, e: (e, 0)),
                  rows, big, rows, big,
                  pl.BlockSpec((HEADS, tn), lambda i, e: (0, i)),
                  pl.BlockSpec((tn, D_MODEL), lambda i, e: (i, 0)),
                  pl.BlockSpec((8, D_MODEL), lambda i, e: (0, 5)),
                  pl.BlockSpec((1, D_MODEL), lambda i, e: (0, 0))],
        out_specs=pl.BlockSpec((tn, D_MODEL), lambda i, e: (i, 0)),
        out_shape=jax.ShapeDtypeStruct((t, D_MODEL), F32),
        scratch_shapes=[pltpu.VMEM((tn, D_MODEL), F32), pltpu.VMEM((eb, tn), F32)],
        compiler_params=_cparams("arbitrary", "arbitrary"),
        name="peer_dense",
    )(h2, u16, v16, s0, s1, e0, e1, tau, x1, mod, final_w)


def _swap_rot_halves(w, n_heads):
    lead = w.shape[0]
    return w.reshape(lead, n_heads, 2, 2, MLA_ROPE // 4)[:, :, :, ::-1, :].reshape(lead, n_heads * MLA_ROPE)


def _permute_w_in(w_in):
    o_qkv = MLA_Q_RANK + MLA_KV_RANK + MLA_ROPE
    o_z = o_qkv + DN_QKV
    o_g = o_z + HEADS * DN_DV
    o_ga = o_g + 4 * HEADS
    kr = w_in[:, MLA_Q_RANK + MLA_KV_RANK:o_qkv]
    misc_pad = jnp.zeros((w_in.shape[0], 1024 - MISC_GATES - 4 * HEADS), w_in.dtype)
    return jnp.concatenate([w_in[:, :o_qkv], _swap_rot_halves(kr, 1), w_in[:, o_g:o_ga], misc_pad,
                            w_in[:, o_z:o_g], w_in[:, o_ga:], w_in[:, o_qkv:o_z]], axis=1).astype(BF16)


def _rope_tables(n, n_heads):
    rows = (jnp.arange(n) // GRID_W).astype(F32)
    cols = (jnp.arange(n) % GRID_W).astype(F32)
    axis_dim = MLA_ROPE // 2
    inv_freq = ROPE_BASE ** (-jnp.arange(0, axis_dim, 2, dtype=F32) / axis_dim)
    ar = rows[:, None] * inv_freq
    ac = cols[:, None] * inv_freq
    cos = jnp.concatenate([jnp.cos(ar), jnp.cos(ar), jnp.cos(ac), jnp.cos(ac)], axis=-1)
    sin = jnp.concatenate([-jnp.sin(ar), jnp.sin(ar), -jnp.sin(ac), jnp.sin(ac)], axis=-1)
    return jnp.tile(cos, (1, n_heads)), jnp.tile(sin, (1, n_heads))


def _mixer_stage(x, ctx, mod, norm1_w, w_in, q_norm_w, kv_norm_w, w_uq, w_ukv, conv_w, a_log, dt_bias, dn_norm_w,
                 w_branch_a, w_branch_b, w_out, norm2_w):
    b, n, d = x.shape
    nctx = ctx.shape[1]

    w_perm = _permute_w_in(w_in)
    uq = w_uq.reshape(MLA_Q_RANK, HEADS, MLA_NOPE + MLA_ROPE)
    uq_rope = uq[:, :, MLA_NOPE:].reshape(MLA_Q_RANK, HEADS * MLA_ROPE)
    wuq = jnp.concatenate([uq[:, :, :MLA_NOPE].reshape(MLA_Q_RANK, HEADS * MLA_NOPE), uq_rope,
                           _swap_rot_halves(uq_rope, HEADS)], axis=1).astype(BF16)
    ukv = w_ukv.reshape(MLA_KV_RANK, HEADS, MLA_NOPE + MLA_V)
    wukv = jnp.concatenate([ukv[:, :, :MLA_NOPE].reshape(MLA_KV_RANK, HEADS * MLA_NOPE),
                            ukv[:, :, MLA_NOPE:].reshape(MLA_KV_RANK, HEADS * MLA_V)], axis=1).astype(BF16)
    pad = jnp.zeros((128 - 2 * HEADS,), F32)
    alog_row = jnp.concatenate([a_log.reshape(-1), pad])[None, :]
    dtb_row = jnp.concatenate([dt_bias.reshape(-1), pad])[None, :]

    p_l = _in_proj(x.reshape(b * n, d), mod, norm1_w[None, :], w_perm, rows_per_batch=n, row0=0).reshape(b, n, D_IN_PAD)
    p_c = _in_proj(ctx.reshape(b * nctx, d), mod, norm1_w[None, :], w_perm,
                   rows_per_batch=b * nctx, row0=b).reshape(b, nctx, D_IN_PAD)

    cos_l, sin_l = _rope_tables(n, HEADS)
    cos_c = jnp.ones((nctx, HEADS * MLA_ROPE), F32)
    sin_c = jnp.zeros((nctx, HEADS * MLA_ROPE), F32)
    qnw, kvnw = q_norm_w[None, :], kv_norm_w[None, :]
    q_l, k_l, v_l = _mla_prep(p_l, qnw, kvnw, wuq, wukv, cos_l, sin_l, with_q=True)
    k_c, v_c = _mla_prep(p_c, qnw, kvnw, wuq, wukv, cos_c, sin_c, with_q=False)
    y_a = _flash(q_l, k_c, v_c, k_l, v_l)

    qkv_c, gates_c = _dn_conv(p_c, conv_w, alog_row, dtb_row)
    qkv_l, gates_l = _dn_conv(p_l, conv_w, alog_row, dtb_row)
    s_zero = jnp.zeros((2, b, HEADS, DN_DK, DN_DV), F32)
    _, _, s_ctx = _dn_scan(s_zero, *_dn_chunk(qkv_c, gates_c))
    o_f, o_b, _ = _dn_scan(s_ctx, *_dn_chunk(qkv_l, gates_l))

    return _merge(y_a, o_f, o_b, p_l, x, mod, dn_norm_w[None, :], norm2_w[None, :],
                  w_branch_a.astype(BF16), w_branch_b.astype(BF16), w_out.astype(BF16))


def _peer_stage(x1, h2, mod, peer_w_q, peer_sub_keys, peer_u, peer_v, final_norm_w):
    b, n, d = x1.shape
    h2f = h2.reshape(b * n, d)
    s0, s1, e0, e1, tau = _peer_topk(h2f, peer_w_q.astype(BF16), peer_sub_keys)
    out = _peer_dense(h2f, peer_u.astype(BF16), peer_v.astype(BF16), s0, s1, e0, e1, tau,
                      x1.reshape(b * n, d), mod, final_norm_w[None, :], rows_per_batch=n)
    return out.reshape(b, n, d)


def kernel(x, c, ctx, c_ctx, w_mod, b_mod, norm1_w, w_in, mla_q_norm_w, mla_kv_norm_w, w_uq, w_ukv, dn_conv_w, dn_a_log, dn_dt_bias, dn_norm_w, w_branch_a, w_branch_b, w_out, norm2_w, peer_w_q, peer_sub_keys, peer_u, peer_v, final_norm_w):
    depth = w_mod.shape[0]
    assert depth == 1, "single-layer block: the context stream is never updated"
    b = x.shape[0]
    cmat = jnp.concatenate([c, c_ctx[None, :], jnp.zeros((8 - b - 1, c.shape[1]), F32)], axis=0)
    mod = _mod(cmat, w_mod[0], b_mod[0][None, :])
    x1, h2 = _mixer_stage(x, ctx, mod, norm1_w[0], w_in[0], mla_q_norm_w[0], mla_kv_norm_w[0], w_uq[0], w_ukv[0],
                          dn_conv_w[0], dn_a_log[0], dn_dt_bias[0], dn_norm_w[0], w_branch_a[0], w_branch_b[0],
                          w_out[0], norm2_w[0])
    return _peer_stage(x1, h2, mod, peer_w_q[0], peer_sub_keys[0], peer_u[0], peer_v[0], final_norm_w)
```

```python
import functools
import math

import jax
import jax.numpy as jnp
import numpy as np
from jax import lax
from jax.experimental import pallas as pl
from jax.experimental.pallas import tpu as pltpu

F32 = jnp.float32
BF16 = jnp.bfloat16
HIGHEST = lax.Precision.HIGHEST

D_MODEL = 2048
EPS = 1e-6
GRID_W = 64
ROPE_BASE = 10000.0
HEADS = 8
MLA_Q_RANK = 512
MLA_KV_RANK = 256
MLA_NOPE = 128
MLA_ROPE = 64
MLA_V = 128
MLA_SCALE = (MLA_NOPE + MLA_ROPE) ** -0.5
QK_PAD = 256
V_PAD = 256
LOG2E = math.log2(math.e)
DN_DK = 128
DN_DV = 128
DN_QKV = HEADS * (2 * DN_DK + DN_DV)
CONV_W = 5
CHUNK = 64
PEER_KEYS = 128
PEER_EXPERTS = PEER_KEYS * PEER_KEYS
PEER_TOPK = 16
D_IN_PAD = 9216

COL_MISC = 0
COL_Z = 1
COL_GA = 1
COL_GB = 2
COL_QKV = 6
MISC_GATES = 896

VMEM_LIMIT = 56 * 1024 * 1024


def _cparams(*sem, flags=None):
    return pltpu.CompilerParams(dimension_semantics=sem, vmem_limit_bytes=VMEM_LIMIT, flags=flags)


def _rms(x):
    return x * lax.rsqrt(jnp.mean(x * x, axis=-1, keepdims=True) + EPS)


def _silu(x):
    return x * jax.nn.sigmoid(x)


def _dot(a, b):
    return jnp.dot(a, b, preferred_element_type=F32)


def _dot_nt(a, b, precision=None):
    return lax.dot_general(a, b, (((1,), (1,)), ((), ())), precision=precision,
                           preferred_element_type=F32)


def _dot_tn(a, b):
    return lax.dot_general(a, b, (((0,), (0,)), ((), ())), preferred_element_type=F32)


def _mod_kernel(c_ref, w_ref, b_ref, o_ref):
    s = _silu(c_ref[...])
    o_ref[...] = jnp.dot(s, w_ref[...], precision=HIGHEST, preferred_element_type=F32) + b_ref[...]


def _mod(cmat, w_mod, b_mod):
    n_out = w_mod.shape[1]
    tn = 1536
    return pl.pallas_call(
        _mod_kernel,
        grid=(n_out // tn,),
        in_specs=[pl.BlockSpec((8, D_MODEL), lambda j: (0, 0)),
                  pl.BlockSpec((D_MODEL, tn), lambda j: (0, j)),
                  pl.BlockSpec((1, tn), lambda j: (0, j))],
        out_specs=pl.BlockSpec((8, tn), lambda j: (0, j)),
        out_shape=jax.ShapeDtypeStruct((8, n_out), F32),
        compiler_params=_cparams("arbitrary"),
        name="mod",
    )(cmat, w_mod, b_mod)


def _in_proj_kernel(x_ref, sh_ref, sc_ref, nw_ref, w_ref, o_ref, h_scr, *, tiles_per_batch, row0):
    @pl.when(pl.program_id(1) == 0)
    def _():
        r = row0 + pl.program_id(0) // tiles_per_batch
        sh = sh_ref[pl.ds(r, 1), :]
        sc = sc_ref[pl.ds(r, 1), :]
        y = _rms(x_ref[...]) * nw_ref[...]
        h_scr[...] = (y * (1.0 + sc) + sh).astype(BF16)

    o_ref[...] = _dot(h_scr[...], w_ref[...])


def _in_proj(x2d, mod, norm_w, w_perm, *, rows_per_batch, row0):
    m = x2d.shape[0]
    tm = min(1024, rows_per_batch)
    tn = 1024
    kern = functools.partial(_in_proj_kernel, tiles_per_batch=rows_per_batch // tm, row0=row0)
    return pl.pallas_call(
        kern,
        grid=(m // tm, D_IN_PAD // tn),
        in_specs=[pl.BlockSpec((tm, D_MODEL), lambda i, j: (i, 0)),
                  pl.BlockSpec((8, D_MODEL), lambda i, j: (0, 0)),
                  pl.BlockSpec((8, D_MODEL), lambda i, j: (0, 1)),
                  pl.BlockSpec((1, D_MODEL), lambda i, j: (0, 0)),
                  pl.BlockSpec((D_MODEL, tn), lambda i, j: (0, j))],
        out_specs=pl.BlockSpec((tm, tn), lambda i, j: (i, j)),
        out_shape=jax.ShapeDtypeStruct((m, D_IN_PAD), F32),
        scratch_shapes=[pltpu.VMEM((tm, D_MODEL), BF16)],
        compiler_params=_cparams("arbitrary", "arbitrary"),
        name="in_proj",
    )(x2d, mod, mod, norm_w, w_perm)


def _mla_prep_kernel(p_ref, qnw_ref, kvnw_ref, wuq_ref, wukv_ref, cq_ref, sq_ref, *out_refs, with_q):
    if with_q:
        q_ref, k_ref, v_ref = out_refs
    else:
        k_ref, v_ref = out_refs
    p = p_ref[...]
    tm = p.shape[0]
    zpad = jnp.zeros((tm, QK_PAD - MLA_NOPE - MLA_ROPE), F32)
    ones_col = (lax.broadcasted_iota(jnp.int32, (tm, V_PAD - MLA_V), 1) == 0).astype(F32)
    ckv = p[:, MLA_Q_RANK:MLA_Q_RANK + MLA_KV_RANK]
    kr = p[:, 768:832]
    kr_sw = p[:, 832:896]
    cosq = cq_ref[...]
    sinq = sq_ref[...]
    kv = _dot((_rms(ckv) * kvnw_ref[...]).astype(BF16), wukv_ref[...])
    kr_rot = kr * cosq[:, :MLA_ROPE] + kr_sw * sinq[:, :MLA_ROPE]
    for h in range(HEADS):
        kh = jnp.concatenate([kv[:, h * MLA_NOPE:(h + 1) * MLA_NOPE], kr_rot, zpad], axis=-1)
        k_ref[h] = kh.astype(BF16)
        vh = kv[:, HEADS * MLA_NOPE + h * MLA_V:HEADS * MLA_NOPE + (h + 1) * MLA_V]
        v_ref[h] = jnp.concatenate([vh, ones_col], axis=-1).astype(BF16)
    if with_q:
        cqv = p[:, :MLA_Q_RANK]
        q = _dot((_rms(cqv) * qnw_ref[...]).astype(BF16), wuq_ref[...])
        n0 = HEADS * MLA_NOPE
        n1 = n0 + HEADS * MLA_ROPE
        qr_rot = q[:, n0:n1] * cosq + q[:, n1:] * sinq
        for h in range(HEADS):
            qh = jnp.concatenate([q[:, h * MLA_NOPE:(h + 1) * MLA_NOPE],
                                  qr_rot[:, h * MLA_ROPE:(h + 1) * MLA_ROPE], zpad], axis=-1)
            q_ref[h] = (qh * (MLA_SCALE * LOG2E)).astype(BF16)


def _mla_prep(p3, qnw, kvnw, wuq, wukv, cos_t, sin_t, *, with_q):
    b, n, _ = p3.shape
    tm = min(256, n)
    qk_spec = pl.BlockSpec((None, HEADS, tm, QK_PAD), lambda bi, i: (bi, 0, i, 0))
    v_spec = pl.BlockSpec((None, HEADS, tm, V_PAD), lambda bi, i: (bi, 0, i, 0))
    qk_shape = jax.ShapeDtypeStruct((b, HEADS, n, QK_PAD), BF16)
    v_shape = jax.ShapeDtypeStruct((b, HEADS, n, V_PAD), BF16)
    out_specs = [qk_spec, v_spec]
    out_shape = [qk_shape, v_shape]
    if with_q:
        out_specs = [qk_spec] + out_specs
        out_shape = [qk_shape] + out_shape
    full = lambda shape: pl.BlockSpec(shape, lambda bi, i: (0,) * len(shape))
    return pl.pallas_call(
        functools.partial(_mla_prep_kernel, with_q=with_q),
        grid=(b, n // tm),
        in_specs=[pl.BlockSpec((None, tm, 1024), lambda bi, i: (bi, i, COL_MISC)),
                  full((1, MLA_Q_RANK)), full((1, MLA_KV_RANK)),
                  full(wuq.shape), full(wukv.shape),
                  pl.BlockSpec((tm, HEADS * MLA_ROPE), lambda bi, i: (i, 0)),
                  pl.BlockSpec((tm, HEADS * MLA_ROPE), lambda bi, i: (i, 0))],
        out_specs=out_specs,
        out_shape=out_shape,
        compiler_params=_cparams("arbitrary", "arbitrary"),
        name="mla_prep_q" if with_q else "mla_prep_kv",
    )(p3, qnw, kvnw, wuq, wukv, cos_t, sin_t)


def _flash_kernel(q_ref, kc_ref, vc_ref, kl_ref, vl_ref, o_ref, s_scr, p_scr, al_scr, m_scr, acc_scr,
                  *, tk, n_chunks, n_sub):
    tq = q_ref.shape[0]
    sub = tq // n_sub
    qs = [q_ref[i * sub:(i + 1) * sub, :] for i in range(n_sub)]

    subs = range(n_sub)

    def scores_to(slot, k):
        for i in subs:
            s_scr[slot, i] = _dot_nt(qs[i], k)

    def softmax_to(s_slot, p_slot):
        ss = [s_scr[s_slot, i] for i in subs]
        m_old = [m_scr[i] for i in subs]
        m_new = [jnp.maximum(m, jnp.max(s, axis=-1, keepdims=True)) for s, m in zip(ss, m_old)]
        for i in subs:
            p_scr[p_slot, i] = jnp.exp2(ss[i] - m_new[i]).astype(BF16)
            al_scr[p_slot, i] = jnp.exp2(m_old[i] - m_new[i])
            m_scr[i] = m_new[i]

    def accumulate(p_slot, v):
        pv = [_dot(p_scr[p_slot, i], v) for i in subs]
        for i in subs:
            acc_scr[i] = al_scr[p_slot, i] * acc_scr[i] + pv[i]

    def chunk(ref, c):
        return ref[pl.ds(pl.multiple_of(c * tk, tk), tk), :]

    s_ctx = [_dot_nt(qs[i], kc_ref[...]) for i in subs]
    scores_to(0, chunk(kl_ref, 0))
    m_ctx = [jnp.max(s, axis=-1, keepdims=True) for s in s_ctx]
    p_ctx = [jnp.exp2(s - m).astype(BF16) for s, m in zip(s_ctx, m_ctx)]
    scores_to(1, chunk(kl_ref, 1))
    for i in subs:
        m_scr[i] = m_ctx[i]
        acc_scr[i] = _dot(p_ctx[i], vc_ref[...])
    softmax_to(0, 0)

    def body(j, _):
        c = 2 * j
        scores_to(0, chunk(kl_ref, c + 2))
        accumulate(0, chunk(vl_ref, c))
        softmax_to(1, 1)
        scores_to(1, chunk(kl_ref, c + 3))
        accumulate(1, chunk(vl_ref, c + 1))
        softmax_to(0, 0)
        return 0

    lax.fori_loop(0, (n_chunks - 2) // 2, body, 0)
    accumulate(0, chunk(vl_ref, n_chunks - 2))
    softmax_to(1, 1)
    accumulate(1, chunk(vl_ref, n_chunks - 1))
    for i in subs:
        acc = acc_scr[i]
        o_ref[i * sub:(i + 1) * sub, :] = (acc[:, :MLA_V] / acc[:, MLA_V:MLA_V + 1]).astype(o_ref.dtype)


def _flash(q, kc, vc, kl, vl):
    b, h, n, _ = q.shape
    nc = kc.shape[2]
    tq = min(512, n)
    tk = min(512, n // 2)
    n_sub = 2
    sub = tq // n_sub
    assert (n // tk) % 2 == 0
    return pl.pallas_call(
        functools.partial(_flash_kernel, tk=tk, n_chunks=n // tk, n_sub=n_sub),
        grid=(b, h, n // tq),
        in_specs=[pl.BlockSpec((None, None, tq, QK_PAD), lambda bi, hi, i: (bi, hi, i, 0)),
                  pl.BlockSpec((None, None, nc, QK_PAD), lambda bi, hi, i: (bi, hi, 0, 0)),
                  pl.BlockSpec((None, None, nc, V_PAD), lambda bi, hi, i: (bi, hi, 0, 0)),
                  pl.BlockSpec((None, None, n, QK_PAD), lambda bi, hi, i: (bi, hi, 0, 0)),
                  pl.BlockSpec((None, None, n, V_PAD), lambda bi, hi, i: (bi, hi, 0, 0))],
        out_specs=pl.BlockSpec((None, tq, MLA_V), lambda bi, hi, i: (bi, i, hi)),
        out_shape=jax.ShapeDtypeStruct((b, n, h * MLA_V), BF16),
        scratch_shapes=[pltpu.VMEM((2, n_sub, sub, tk), F32), pltpu.VMEM((2, n_sub, sub, tk), BF16),
                        pltpu.VMEM((2, n_sub, sub, 1), F32), pltpu.VMEM((n_sub, sub, 1), F32),
                        pltpu.VMEM((n_sub, sub, V_PAD), F32)],
        compiler_params=_cparams("arbitrary", "arbitrary", "arbitrary"),
        name="flash",
    )(q, kc, vc, kl, vl)


def _dn_conv_kernel(cur_ref, prev_ref, next_ref, cw_ref, gl_ref, alog_ref, dtb_ref, o_ref, g_ref, scr, *, n_tiles):
    i = pl.program_id(1)
    part = pl.program_id(2)
    tm = cur_ref.shape[0]
    scr[pl.ds(0, 8), :] = jnp.where(i > 0, prev_ref[...], 0.0)
    scr[pl.ds(8, tm), :] = cur_ref[...]
    scr[pl.ds(8 + tm, 8), :] = jnp.where(i < n_tiles - 1, next_ref[...], 0.0)
    cw = cw_ref[...]
    y = scr[pl.ds(8 - CONV_W // 2, tm), :] * cw[0:1, :]
    for w in range(1, CONV_W):
        y = y + scr[pl.ds(8 - CONV_W // 2 + w, tm), :] * cw[w:w + 1, :]
    y = _silu(y)

    @pl.when(part == 2)
    def _():
        o_ref[...] = y

    @pl.when(part < 2)
    def _():
        scale = jnp.where(part == 0, DN_DK ** -0.5, 1.0).astype(F32)
        for h in range(HEADS):
            yh = y[:, h * DN_DK:(h + 1) * DN_DK]
            yh = yh * lax.rsqrt(jnp.sum(yh * yh, axis=-1, keepdims=True) + EPS)
            o_ref[:, h * DN_DK:(h + 1) * DN_DK] = yh * scale

    @pl.when(part == 0)
    def _():
        val = gl_ref[...]
        lane = lax.broadcasted_iota(jnp.int32, val.shape, 1)
        g = -jnp.exp(alog_ref[...]) * jax.nn.softplus(val + dtb_ref[...])
        g_ref[...] = jnp.where(lane < 2 * HEADS, g, jax.nn.sigmoid(val))


def _dn_conv(p3, conv_w, alog_row, dtb_row):
    b, n, _ = p3.shape
    tm = min(256, n)
    nt = n // tm
    r8 = tm // 8
    return pl.pallas_call(
        functools.partial(_dn_conv_kernel, n_tiles=nt),
        grid=(b, nt, 3),
        in_specs=[pl.BlockSpec((None, tm, 1024), lambda bi, i, p: (bi, i, COL_QKV + p)),
                  pl.BlockSpec((None, 8, 1024), lambda bi, i, p: (bi, jnp.maximum(i * r8 - 1, 0), COL_QKV + p)),
                  pl.BlockSpec((None, 8, 1024), lambda bi, i, p: (bi, jnp.minimum((i + 1) * r8, nt * r8 - 1), COL_QKV + p)),
                  pl.BlockSpec((CONV_W, 1024), lambda bi, i, p: (0, p)),
                  pl.BlockSpec((None, tm, 128), lambda bi, i, p: (bi, i, MISC_GATES // 128)),
                  pl.BlockSpec((1, 128), lambda bi, i, p: (0, 0)),
                  pl.BlockSpec((1, 128), lambda bi, i, p: (0, 0))],
        out_specs=[pl.BlockSpec((None, tm, 1024), lambda bi, i, p: (bi, i, p)),
                   pl.BlockSpec((None, tm, 128), lambda bi, i, p: (bi, i, 0))],
        out_shape=[jax.ShapeDtypeStruct((b, n, DN_QKV), F32),
                   jax.ShapeDtypeStruct((b, n, 128), F32)],
        scratch_shapes=[pltpu.VMEM((tm + 16, 1024), F32)],
        compiler_params=_cparams("arbitrary", "arbitrary", "arbitrary"),
        name="dn_conv",
    )(p3, p3, p3, conv_w, p3, alog_row, dtb_row)


def _dn_chunk_kernel(q_ref, k_ref, v_ref, g_ref, *out_refs):
    ri = lax.broadcasted_iota(jnp.int32, (CHUNK, CHUNK), 0)
    ci = lax.broadcasted_iota(jnp.int32, (CHUNK, CHUNK), 1)
    eye = (ri == ci).astype(F32)
    incl = (ri >= ci, ri <= ci)
    strict = (ri > ci, ri < ci)
    gates = g_ref[...]
    gc = [jnp.dot(incl[d].astype(F32), gates, precision=HIGHEST, preferred_element_type=F32) for d in range(2)]
    gct = [g.T for g in gc]
    chains = [(h, d) for h in range(HEADS) for d in range(2)]
    col_of = lambda h: slice(h * DN_DK, (h + 1) * DN_DK)
    q = [q_ref[:, col_of(h)] for h in range(HEADS)]
    k = [k_ref[:, col_of(h)] for h in range(HEADS)]
    k16 = [kh.astype(BF16) for kh in k]
    kk = [_dot_nt(k16[h], k16[h]) for h in range(HEADS)]
    qk = [_dot_nt(q[h].astype(BF16), k16[h]) for h in range(HEADS)]
    gcol, glast, beta, decay, t, pw = {}, {}, {}, {}, {}, {}
    for h, d in chains:
        c = (h, d)
        col = d * HEADS + h
        last = CHUNK - 1 if d == 0 else 0
        gcol[c] = gc[d][:, col:col + 1]
        glast[c] = gc[d][last:last + 1, col:col + 1]
        beta[c] = gates[:, 2 * HEADS + col:2 * HEADS + col + 1]
        decay[c] = jnp.exp(jnp.where(incl[d], gcol[c] - gct[d][col:col + 1, :], -jnp.inf))
        pw[c] = jnp.where(strict[d], -(beta[c] * kk[h] * decay[c]), 0.0)
        t[c] = eye + pw[c]
    for _ in range(5):
        for c in chains:
            pw16 = pw[c].astype(BF16)
            pw[c] = _dot(pw16, pw16)
        for c in chains:
            t[c] = t[c] + _dot(t[c].astype(BF16), pw[c].astype(BF16))
    sol = {}
    for h, d in chains:
        c = (h, d)
        vh = v_ref[:, col_of(h)]
        rhs = jnp.concatenate([vh * beta[c], k[h] * beta[c] * jnp.exp(gcol[c])], axis=-1).astype(BF16)
        sol[c] = _dot(t[c].astype(BF16), rhs)
    for h, d in chains:
        c = (h, d)
        cs = col_of(h)
        u_ref, w_ref, qg_ref, kd_ref, att_ref, egl_ref = out_refs[6 * d:6 * d + 6]
        u_ref[:, cs] = sol[c][:, :DN_DV]
        w_ref[:, cs] = sol[c][:, DN_DV:].astype(BF16)
        qg_ref[:, cs] = (q[h] * jnp.exp(gcol[c])).astype(BF16)
        kd_ref[:, cs] = (k[h] * jnp.exp(glast[c] - gcol[c])).astype(BF16)
        att_ref[:, h * CHUNK:(h + 1) * CHUNK] = jnp.where(incl[d], qk[h] * decay[c], 0.0).astype(BF16)
        egl_ref[h:h + 1, :] = jnp.broadcast_to(jnp.exp(glast[c]), (1, 128))


def _dn_chunk(qkvn, gates):
    b, n, _ = qkvn.shape
    nchunks = n // CHUNK
    tm = CHUNK
    hd = HEADS * DN_DK
    row = lambda w: pl.BlockSpec((None, tm, w), lambda bi, i: (bi, i, 0))
    one_dir_specs = [row(hd), row(hd), row(hd), row(hd), row(HEADS * CHUNK),
                     pl.BlockSpec((None, HEADS, 128), lambda bi, i: (bi, i, 0))]
    one_dir_shapes = [jax.ShapeDtypeStruct((b, n, hd), F32),
                      jax.ShapeDtypeStruct((b, n, hd), BF16),
                      jax.ShapeDtypeStruct((b, n, hd), BF16),
                      jax.ShapeDtypeStruct((b, n, hd), BF16),
                      jax.ShapeDtypeStruct((b, n, HEADS * CHUNK), BF16),
                      jax.ShapeDtypeStruct((b, nchunks * HEADS, 128), F32)]
    outs = pl.pallas_call(
        _dn_chunk_kernel,
        grid=(b, nchunks),
        in_specs=[pl.BlockSpec((None, tm, hd), lambda bi, i: (bi, i, 0)),
                  pl.BlockSpec((None, tm, hd), lambda bi, i: (bi, i, 1)),
                  pl.BlockSpec((None, tm, hd), lambda bi, i: (bi, i, 2)),
                  pl.BlockSpec((None, tm, 128), lambda bi, i: (bi, i, 0))],
        out_specs=one_dir_specs * 2,
        out_shape=one_dir_shapes * 2,
        compiler_params=_cparams("arbitrary", "arbitrary"),
        name="dn_chunk",
    )(qkvn, qkvn, qkvn, gates)
    return outs[:6], outs[6:]


def _dn_scan_kernel(s0_ref, *refs, n_batch):
    ins = refs[:12]
    of_ref, ob_ref, sf_ref, s_scr = refs[12:]
    step = pl.program_id(0)

    @pl.when(step == 0)
    def _():
        s_scr[...] = s0_ref[...]

    chains = [(d, b, h) for d in range(2) for b in range(n_batch) for h in range(HEADS)]
    col_of = lambda h: slice(h * DN_DK, (h + 1) * DN_DK)
    s16, ws, qs, v16 = {}, {}, {}, {}
    for c in chains:
        s16[c] = s_scr[c].astype(BF16)
    for c in chains:
        d, b, h = c
        ws[c] = _dot(ins[6 * d + 1][b, :, col_of(h)], s16[c])
        qs[c] = _dot(ins[6 * d + 2][b, :, col_of(h)], s16[c])
    for c in chains:
        d, b, h = c
        v16[c] = (ins[6 * d][b, :, col_of(h)] - ws[c]).astype(BF16)
    for c in chains:
        d, b, h = c
        att = ins[6 * d + 4][b, :, h * CHUNK:(h + 1) * CHUNK]
        o_ref = of_ref if d == 0 else ob_ref
        o_ref[b, :, col_of(h)] = qs[c] + _dot(att, v16[c])
        s_scr[c] = s_scr[c] * ins[6 * d + 5][b, h:h + 1, :] + _dot_tn(ins[6 * d + 3][b, :, col_of(h)], v16[c])

    @pl.when(step == pl.num_programs(0) - 1)
    def _():
        sf_ref[...] = s_scr[...]


def _dn_scan(s0, fwd, bwd):
    b, n, hd = fwd[0].shape
    nchunks = n // CHUNK
    f_idx = lambda s: (0, s, 0)
    b_idx = lambda s: (0, nchunks - 1 - s, 0)

    def specs(idx):
        return [pl.BlockSpec((b, CHUNK, hd), idx)] * 4 + [pl.BlockSpec((b, CHUNK, HEADS * CHUNK), idx),
                                                          pl.BlockSpec((b, HEADS, 128), idx)]

    state_spec = pl.BlockSpec(s0.shape, lambda s: (0,) * 5)
    o_f, o_b, s_fin = pl.pallas_call(
        functools.partial(_dn_scan_kernel, n_batch=b),
        grid=(nchunks,),
        in_specs=[state_spec] + specs(f_idx) + specs(b_idx),
        out_specs=[pl.BlockSpec((b, CHUNK, hd), f_idx), pl.BlockSpec((b, CHUNK, hd), b_idx), state_spec],
        out_shape=[jax.ShapeDtypeStruct((b, n, hd), F32), jax.ShapeDtypeStruct((b, n, hd), F32),
                   jax.ShapeDtypeStruct(s0.shape, F32)],
        scratch_shapes=[pltpu.VMEM(s0.shape, F32)],
        compiler_params=_cparams("arbitrary"),
        name="dn_scan",
    )(s0, *fwd, *bwd)
    return o_f, o_b, s_fin


def _merge_kernel(ya_ref, of_ref, ob_ref, z_ref, ga_ref, gb_ref, x_ref, g1_ref, sh2_ref, sc2_ref,
                  dnw_ref, n2w_ref, wa_ref, wb_ref, wo_ref, x1_ref, h2_ref):
    b = pl.program_id(0)
    o = of_ref[...] + ob_ref[...]
    z = z_ref[...]
    dnw = dnw_ref[...]
    parts = []
    for h in range(HEADS):
        cs = slice(h * DN_DV, (h + 1) * DN_DV)
        parts.append(_rms(o[:, cs]) * dnw * _silu(z[:, cs]))
    yb = jnp.concatenate(parts, axis=-1).astype(BF16)
    t = (jax.nn.sigmoid(ga_ref[...]) * _dot(ya_ref[...], wa_ref[...])
         + jax.nn.sigmoid(gb_ref[...]) * _dot(yb, wb_ref[...]))
    y = _dot(t.astype(BF16), wo_ref[...])
    x1 = x_ref[...] + g1_ref[pl.ds(b, 1), :] * y
    x1_ref[...] = x1
    h2 = _rms(x1) * n2w_ref[...] * (1.0 + sc2_ref[pl.ds(b, 1), :]) + sh2_ref[pl.ds(b, 1), :]
    h2_ref[...] = h2.astype(BF16)


def _merge(ya, o_f, o_b, p3, x, mod, dn_norm_w, norm2_w, wa, wb, wo):
    b, n, _ = x.shape
    tm = min(256, n)
    row = lambda w, j=0: pl.BlockSpec((None, tm, w), lambda bi, i: (bi, i, j))
    modrow = lambda j: pl.BlockSpec((8, D_MODEL), lambda bi, i: (0, j))
    const = lambda shape: pl.BlockSpec(shape, lambda bi, i: (0,) * len(shape), pipeline_mode=pl.Buffered(1))
    return pl.pallas_call(
        _merge_kernel,
        grid=(b, n // tm),
        in_specs=[row(1024), row(1024), row(1024), row(1024, COL_Z), row(D_MODEL, COL_GA), row(D_MODEL, COL_GB),
                  row(D_MODEL), modrow(2), modrow(3), modrow(4),
                  const((1, DN_DV)), const((1, D_MODEL)),
                  const(wa.shape), const(wb.shape), const(wo.shape)],
        out_specs=[row(D_MODEL), row(D_MODEL)],
        out_shape=[jax.ShapeDtypeStruct((b, n, D_MODEL), F32), jax.ShapeDtypeStruct((b, n, D_MODEL), BF16)],
        compiler_params=_cparams("arbitrary", "arbitrary"),
        name="merge",
    )(ya, o_f, o_b, p3, p3, p3, x, mod, mod, mod, dn_norm_w, norm2_w, wa, wb, wo)


def _top16_rows(w, out_scr):
    for n in range(PEER_TOPK):
        m = jnp.max(w, axis=0, keepdims=True)
        out_scr[n:n + 1, :] = m
        w = jnp.where(w == m, -jnp.inf, w)


_PAIR_COUNTS = tuple(PEER_TOPK // (m + 1) for m in range(PEER_TOPK))
_N_CAND = sum(_PAIR_COUNTS)
_CAND_ROWS = -(-_N_CAND // 8) * 8


def _peer_topk_kernel(h2_ref, wq_ref, sk_ref, cut_ref, s1_ref, e0_ref, e1_ref, q_scr, a_scr, b_scr, c_scr):
    q_scr[...] = _dot(h2_ref[...], wq_ref[...])
    tn = q_scr.shape[0]

    def head_body(h, _):
        off0 = pl.multiple_of(h * 2 * PEER_KEYS, PEER_KEYS)
        off1 = pl.multiple_of(h * 2 * PEER_KEYS + PEER_KEYS, PEER_KEYS)
        s0 = _dot_nt(sk_ref[0], q_scr[:, pl.ds(off0, PEER_KEYS)], precision=HIGHEST)
        s1 = _dot_nt(sk_ref[1], q_scr[:, pl.ds(off1, PEER_KEYS)], precision=HIGHEST)
        _top16_rows(s0, a_scr)
        _top16_rows(s1, b_scr)
        btop = b_scr[...]
        c_scr[_CAND_ROWS - 8:, :] = jnp.full((8, tn), -jnp.inf, F32)
        off = 0
        for m, cnt in enumerate(_PAIR_COUNTS):
            c_scr[off:off + cnt, :] = a_scr[m:m + 1, :] + btop[:cnt, :]
            off += cnt
        cand = c_scr[...]
        zsum = jnp.zeros((1, tn), F32)
        for n in range(PEER_TOPK):
            best = jnp.max(cand, axis=0, keepdims=True)
            if n == 0:
                top = best
            zsum = zsum + jnp.exp(best - top)
            cand = jnp.where(cand == best, -jnp.inf, cand)
        tau = best
        cut = jnp.full(s0.shape, jnp.inf, F32)
        for n in range(PEER_TOPK):
            bn = btop[n:n + 1, :]
            cut = jnp.minimum(cut, jnp.where(s0 + bn >= tau, bn, jnp.inf))
        cut_ref[h] = cut
        s1_ref[h] = s1
        e0_ref[h] = jnp.exp(s0 - a_scr[0:1, :]) / zsum
        e1_ref[h] = jnp.exp(s1 - btop[0:1, :])
        return 0

    lax.fori_loop(0, HEADS, head_body, 0)


def _peer_topk(h2, wq, sub_keys):
    t = h2.shape[0]
    tn = min(256, t)
    big = pl.BlockSpec((HEADS, PEER_KEYS, tn), lambda i: (0, 0, i))
    big_shape = jax.ShapeDtypeStruct((HEADS, PEER_KEYS, t), F32)
    return pl.pallas_call(
        _peer_topk_kernel,
        grid=(t // tn,),
        in_specs=[pl.BlockSpec((tn, D_MODEL), lambda i: (i, 0)),
                  pl.BlockSpec(wq.shape, lambda i: (0, 0), pipeline_mode=pl.Buffered(1)),
                  pl.BlockSpec(sub_keys.shape, lambda i: (0, 0, 0))],
        out_specs=[big, big, big, big],
        out_shape=[big_shape, big_shape, big_shape, big_shape],
        scratch_shapes=[pltpu.VMEM((tn, wq.shape[1]), F32), pltpu.VMEM((PEER_TOPK, tn), F32),
                        pltpu.VMEM((PEER_TOPK, tn), F32), pltpu.VMEM((_CAND_ROWS, tn), F32)],
        compiler_params=_cparams("arbitrary"),
        name="peer_topk",
    )(h2, wq, sub_keys)


def _peer_dense_kernel(h2_ref, u_ref, vt_ref, cut_ref, s1_ref, e0_ref, e1_ref, x1_ref, g2_ref, fw_ref,
                       o_ref, acc, g_scr, *, tiles_per_batch):
    e = pl.program_id(1)
    eb = u_ref.shape[0]
    tn = h2_ref.shape[0]
    piece = 2 * PEER_KEYS
    jh = PEER_KEYS // 2

    @pl.when(e == 0)
    def _():
        acc[...] = jnp.zeros_like(acc)

    def route_and_gate(p, a):
        for lg in range(tn // 128):
            ls = slice(lg * 128, (lg + 1) * 128)
            for half in range(2):
                js = slice(half * jh, (half + 1) * jh)
                tots = [jnp.zeros((jh, 128), F32), jnp.zeros((jh, 128), F32)]
                for h in range(HEADS):
                    s1h = s1_ref[h, js, ls]
                    e1h = e1_ref[h, js, ls]
                    for ii in range(2):
                        il = 2 * p + ii
                        sel = s1h >= cut_ref[h, il:il + 1, ls]
                        tots[ii] = tots[ii] + jnp.where(sel, e1h, 0.0) * e0_ref[h, il:il + 1, ls]
                for ii in range(2):
                    r0 = ii * PEER_KEYS + half * jh
                    ap = a[r0:r0 + jh, ls]
                    gated = 0.5 * ap * (1.0 + lax.erf(ap * (2.0 ** -0.5))) * tots[ii]
                    g_scr[p * piece + r0:p * piece + r0 + jh, ls] = gated.astype(BF16)

    prev = None
    for p in range(eb // piece):
        a = _dot_nt(u_ref[p * piece:(p + 1) * piece, :], h2_ref[...])
        if prev is not None:
            route_and_gate(p - 1, prev)
        prev = a
    route_and_gate(eb // piece - 1, prev)
    acc[...] += _dot(vt_ref[...], g_scr[...])

    @pl.when(e == pl.num_programs(1) - 1)
    def _():
        b = pl.program_id(0) // tiles_per_batch
        xo = x1_ref[...] + g2_ref[pl.ds(b, 1), :] * acc[...].T
        o_ref[...] = _rms(xo) * fw_ref[...]


def _peer_dense(h2, u16, vt16, cut, s1, e0, e1, x1, mod, final_w, *, rows_per_batch):
    t = h2.shape[0]
    tn = min(512, rows_per_batch)
    eb = 8 * PEER_KEYS
    big = pl.BlockSpec((HEADS, PEER_KEYS, tn), lambda i, e: (0, 0, i))
    rows = pl.BlockSpec((HEADS, eb // PEER_KEYS, tn), lambda i, e: (0, e, i))
    return pl.pallas_call(
        functools.partial(_peer_dense_kernel, tiles_per_batch=rows_per_batch // tn),
        grid=(t // tn, PEER_EXPERTS // eb),
        in_specs=[pl.BlockSpec((tn, D_MODEL), lambda i, e: (i, 0), pipeline_mode=pl.Buffered(1)),
                  pl.BlockSpec((eb, D_MODEL), lambda i, e: (e, 0)),
                  pl.BlockSpec((D_MODEL, eb), lambda i, e: (0, e)),
                  rows, big, rows, big,
                  pl.BlockSpec((tn, D_MODEL), lambda i, e: (i, 0), pipeline_mode=pl.Buffered(1)),
                  pl.BlockSpec((8, D_MODEL), lambda i, e: (0, 5)),
                  pl.BlockSpec((1, D_MODEL), lambda i, e: (0, 0))],
        out_specs=pl.BlockSpec((tn, D_MODEL), lambda i, e: (i, 0)),
        out_shape=jax.ShapeDtypeStruct((t, D_MODEL), F32),
        scratch_shapes=[pltpu.VMEM((D_MODEL, tn), F32), pltpu.VMEM((eb, tn), BF16)],
        compiler_params=_cparams("arbitrary", "arbitrary"),
        name="peer_dense",
    )(h2, u16, vt16, cut, s1, e0, e1, x1, mod, final_w)


def _swap_rot_halves(w, n_heads):
    lead = w.shape[0]
    return w.reshape(lead, n_heads, 2, 2, MLA_ROPE // 4)[:, :, :, ::-1, :].reshape(lead, n_heads * MLA_ROPE)


def _permute_w_in(w_in):
    o_qkv = MLA_Q_RANK + MLA_KV_RANK + MLA_ROPE
    o_z = o_qkv + DN_QKV
    o_g = o_z + HEADS * DN_DV
    o_ga = o_g + 4 * HEADS
    kr = w_in[:, MLA_Q_RANK + MLA_KV_RANK:o_qkv]
    misc_pad = jnp.zeros((w_in.shape[0], 1024 - MISC_GATES - 4 * HEADS), w_in.dtype)
    return jnp.concatenate([w_in[:, :o_qkv], _swap_rot_halves(kr, 1), w_in[:, o_g:o_ga], misc_pad,
                            w_in[:, o_z:o_g], w_in[:, o_ga:], w_in[:, o_qkv:o_z]], axis=1).astype(BF16)


def _rope_tables(n, n_heads):
    rows = (jnp.arange(n) // GRID_W).astype(F32)
    cols = (jnp.arange(n) % GRID_W).astype(F32)
    axis_dim = MLA_ROPE // 2
    inv_freq = ROPE_BASE ** (-jnp.arange(0, axis_dim, 2, dtype=F32) / axis_dim)
    ar = rows[:, None] * inv_freq
    ac = cols[:, None] * inv_freq
    cos = jnp.concatenate([jnp.cos(ar), jnp.cos(ar), jnp.cos(ac), jnp.cos(ac)], axis=-1)
    sin = jnp.concatenate([-jnp.sin(ar), jnp.sin(ar), -jnp.sin(ac), jnp.sin(ac)], axis=-1)
    return jnp.tile(cos, (1, n_heads)), jnp.tile(sin, (1, n_heads))


def _mixer_stage(x, ctx, mod, norm1_w, w_in, q_norm_w, kv_norm_w, w_uq, w_ukv, conv_w, a_log, dt_bias, dn_norm_w,
                 w_branch_a, w_branch_b, w_out, norm2_w):
    b, n, d = x.shape
    nctx = ctx.shape[1]

    w_perm = _permute_w_in(w_in)
    uq = w_uq.reshape(MLA_Q_RANK, HEADS, MLA_NOPE + MLA_ROPE)
    uq_rope = uq[:, :, MLA_NOPE:].reshape(MLA_Q_RANK, HEADS * MLA_ROPE)
    wuq = jnp.concatenate([uq[:, :, :MLA_NOPE].reshape(MLA_Q_RANK, HEADS * MLA_NOPE), uq_rope,
                           _swap_rot_halves(uq_rope, HEADS)], axis=1).astype(BF16)
    ukv = w_ukv.reshape(MLA_KV_RANK, HEADS, MLA_NOPE + MLA_V)
    wukv = jnp.concatenate([ukv[:, :, :MLA_NOPE].reshape(MLA_KV_RANK, HEADS * MLA_NOPE),
                            ukv[:, :, MLA_NOPE:].reshape(MLA_KV_RANK, HEADS * MLA_V)], axis=1).astype(BF16)
    pad = jnp.zeros((128 - 2 * HEADS,), F32)
    alog_row = jnp.concatenate([a_log.reshape(-1), pad])[None, :]
    dtb_row = jnp.concatenate([dt_bias.reshape(-1), pad])[None, :]

    p_l = _in_proj(x.reshape(b * n, d), mod, norm1_w[None, :], w_perm, rows_per_batch=n, row0=0).reshape(b, n, D_IN_PAD)
    p_c = _in_proj(ctx.reshape(b * nctx, d), mod, norm1_w[None, :], w_perm,
                   rows_per_batch=b * nctx, row0=b).reshape(b, nctx, D_IN_PAD)

    cos_l, sin_l = _rope_tables(n, HEADS)
    cos_c = jnp.ones((nctx, HEADS * MLA_ROPE), F32)
    sin_c = jnp.zeros((nctx, HEADS * MLA_ROPE), F32)
    qnw, kvnw = q_norm_w[None, :], kv_norm_w[None, :]
    q_l, k_l, v_l = _mla_prep(p_l, qnw, kvnw, wuq, wukv, cos_l, sin_l, with_q=True)
    k_c, v_c = _mla_prep(p_c, qnw, kvnw, wuq, wukv, cos_c, sin_c, with_q=False)
    y_a = _flash(q_l, k_c, v_c, k_l, v_l)

    qkv_c, gates_c = _dn_conv(p_c, conv_w, alog_row, dtb_row)
    qkv_l, gates_l = _dn_conv(p_l, conv_w, alog_row, dtb_row)
    s_zero = jnp.zeros((2, b, HEADS, DN_DK, DN_DV), F32)
    _, _, s_ctx = _dn_scan(s_zero, *_dn_chunk(qkv_c, gates_c))
    o_f, o_b, _ = _dn_scan(s_ctx, *_dn_chunk(qkv_l, gates_l))

    return _merge(y_a, o_f, o_b, p_l, x, mod, dn_norm_w[None, :], norm2_w[None, :],
                  w_branch_a.astype(BF16), w_branch_b.astype(BF16), w_out.astype(BF16))


def _peer_stage(x1, h2, mod, peer_w_q, peer_sub_keys, peer_u, peer_v, final_norm_w):
    b, n, d = x1.shape
    h2f = h2.reshape(b * n, d)
    cut, s1, e0, e1 = _peer_topk(h2f, peer_w_q.astype(BF16), peer_sub_keys)
    out = _peer_dense(h2f, peer_u.astype(BF16), peer_v.T.astype(BF16), cut, s1, e0, e1,
                      x1.reshape(b * n, d), mod, final_norm_w[None, :], rows_per_batch=n)
    return out.reshape(b, n, d)


def kernel(x, c, ctx, c_ctx, w_mod, b_mod, norm1_w, w_in, mla_q_norm_w, mla_kv_norm_w, w_uq, w_ukv, dn_conv_w, dn_a_log, dn_dt_bias, dn_norm_w, w_branch_a, w_branch_b, w_out, norm2_w, peer_w_q, peer_sub_keys, peer_u, peer_v, final_norm_w):
    depth = w_mod.shape[0]
    assert depth == 1, "single-layer block: the context stream is never updated"
    b = x.shape[0]
    cmat = jnp.concatenate([c, c_ctx[None, :], jnp.zeros((8 - b - 1, c.shape[1]), F32)], axis=0)
    mod = _mod(cmat, w_mod[0], b_mod[0][None, :])
    x1, h2 = _mixer_stage(x, ctx, mod, norm1_w[0], w_in[0], mla_q_norm_w[0], mla_kv_norm_w[0], w_uq[0], w_ukv[0],
                          dn_conv_w[0], dn_a_log[0], dn_dt_bias[0], dn_norm_w[0], w_branch_a[0], w_branch_b[0],
                          w_out[0], norm2_w[0])
    return _peer_stage(x1, h2, mod, peer_w_q[0], peer_sub_keys[0], peer_u[0], peer_v[0], final_norm_w)
```

```python
import functools
import math

import jax
import jax.numpy as jnp
import numpy as np
from jax import lax
from jax.experimental import pallas as pl
from jax.experimental.pallas import tpu as pltpu

F32 = jnp.float32
BF16 = jnp.bfloat16
HIGHEST = lax.Precision.HIGHEST

D_MODEL = 2048
EPS = 1e-6
GRID_W = 64
ROPE_BASE = 10000.0
HEADS = 8
MLA_Q_RANK = 512
MLA_KV_RANK = 256
MLA_NOPE = 128
MLA_ROPE = 64
MLA_V = 128
MLA_SCALE = (MLA_NOPE + MLA_ROPE) ** -0.5
QK_PAD = 256
V_PAD = 256
LOG2E = math.log2(math.e)
DN_DK = 128
DN_DV = 128
DN_QKV = HEADS * (2 * DN_DK + DN_DV)
CONV_W = 5
CHUNK = 64
PEER_KEYS = 128
PEER_EXPERTS = PEER_KEYS * PEER_KEYS
PEER_TOPK = 16
D_IN_PAD = 9216

COL_MISC = 0
COL_Z = 1
COL_GA = 1
COL_GB = 2
COL_QKV = 6
MISC_GATES = 896

VMEM_LIMIT = 56 * 1024 * 1024


def _cparams(*sem, flags=None):
    return pltpu.CompilerParams(dimension_semantics=sem, vmem_limit_bytes=VMEM_LIMIT, flags=flags)


def _rms(x):
    return x * lax.rsqrt(jnp.mean(x * x, axis=-1, keepdims=True) + EPS)


def _silu(x):
    return x * jax.nn.sigmoid(x)


def _dot(a, b):
    return jnp.dot(a, b, preferred_element_type=F32)


def _dot_nt(a, b, precision=None):
    return lax.dot_general(a, b, (((1,), (1,)), ((), ())), precision=precision,
                           preferred_element_type=F32)


def _dot_tn(a, b):
    return lax.dot_general(a, b, (((0,), (0,)), ((), ())), preferred_element_type=F32)


def _mod_kernel(c_ref, w_ref, b_ref, o_ref):
    s = _silu(c_ref[...])
    o_ref[...] = jnp.dot(s, w_ref[...], precision=HIGHEST, preferred_element_type=F32) + b_ref[...]


def _mod(cmat, w_mod, b_mod):
    n_out = w_mod.shape[1]
    tn = 1536
    return pl.pallas_call(
        _mod_kernel,
        grid=(n_out // tn,),
        in_specs=[pl.BlockSpec((8, D_MODEL), lambda j: (0, 0)),
                  pl.BlockSpec((D_MODEL, tn), lambda j: (0, j)),
                  pl.BlockSpec((1, tn), lambda j: (0, j))],
        out_specs=pl.BlockSpec((8, tn), lambda j: (0, j)),
        out_shape=jax.ShapeDtypeStruct((8, n_out), F32),
        compiler_params=_cparams("arbitrary"),
        name="mod",
    )(cmat, w_mod, b_mod)


def _in_proj_kernel(x_ref, sh_ref, sc_ref, nw_ref, w_ref, o_ref, h_scr, *, tiles_per_batch, row0):
    @pl.when(pl.program_id(1) == 0)
    def _():
        r = row0 + pl.program_id(0) // tiles_per_batch
        sh = sh_ref[pl.ds(r, 1), :]
        sc = sc_ref[pl.ds(r, 1), :]
        y = _rms(x_ref[...]) * nw_ref[...]
        h_scr[...] = (y * (1.0 + sc) + sh).astype(BF16)

    o_ref[...] = _dot(h_scr[...], w_ref[...])


def _in_proj(x2d, mod, norm_w, w_perm, *, rows_per_batch, row0):
    m = x2d.shape[0]
    tm = min(1024, rows_per_batch)
    tn = 1024
    kern = functools.partial(_in_proj_kernel, tiles_per_batch=rows_per_batch // tm, row0=row0)
    return pl.pallas_call(
        kern,
        grid=(m // tm, D_IN_PAD // tn),
        in_specs=[pl.BlockSpec((tm, D_MODEL), lambda i, j: (i, 0)),
                  pl.BlockSpec((8, D_MODEL), lambda i, j: (0, 0)),
                  pl.BlockSpec((8, D_MODEL), lambda i, j: (0, 1)),
                  pl.BlockSpec((1, D_MODEL), lambda i, j: (0, 0)),
                  pl.BlockSpec((D_MODEL, tn), lambda i, j: (0, j))],
        out_specs=pl.BlockSpec((tm, tn), lambda i, j: (i, j)),
        out_shape=jax.ShapeDtypeStruct((m, D_IN_PAD), F32),
        scratch_shapes=[pltpu.VMEM((tm, D_MODEL), BF16)],
        compiler_params=_cparams("arbitrary", "arbitrary"),
        name="in_proj",
    )(x2d, mod, mod, norm_w, w_perm)


def _mla_prep_kernel(p_ref, qnw_ref, kvnw_ref, wuq_ref, wukv_ref, cq_ref, sq_ref, *out_refs, with_q):
    if with_q:
        q_ref, k_ref, v_ref = out_refs
    else:
        k_ref, v_ref = out_refs
    p = p_ref[...]
    tm = p.shape[0]
    zpad = jnp.zeros((tm, QK_PAD - MLA_NOPE - MLA_ROPE), F32)
    ones_col = (lax.broadcasted_iota(jnp.int32, (tm, V_PAD - MLA_V), 1) == 0).astype(F32)
    ckv = p[:, MLA_Q_RANK:MLA_Q_RANK + MLA_KV_RANK]
    kr = p[:, 768:832]
    kr_sw = p[:, 832:896]
    cosq = cq_ref[...]
    sinq = sq_ref[...]
    kv = _dot((_rms(ckv) * kvnw_ref[...]).astype(BF16), wukv_ref[...])
    kr_rot = kr * cosq[:, :MLA_ROPE] + kr_sw * sinq[:, :MLA_ROPE]
    for h in range(HEADS):
        kh = jnp.concatenate([kv[:, h * MLA_NOPE:(h + 1) * MLA_NOPE], kr_rot, zpad], axis=-1)
        k_ref[h] = kh.astype(BF16)
        vh = kv[:, HEADS * MLA_NOPE + h * MLA_V:HEADS * MLA_NOPE + (h + 1) * MLA_V]
        v_ref[h] = jnp.concatenate([vh, ones_col], axis=-1).astype(BF16)
    if with_q:
        cqv = p[:, :MLA_Q_RANK]
        q = _dot((_rms(cqv) * qnw_ref[...]).astype(BF16), wuq_ref[...])
        n0 = HEADS * MLA_NOPE
        n1 = n0 + HEADS * MLA_ROPE
        qr_rot = q[:, n0:n1] * cosq + q[:, n1:] * sinq
        for h in range(HEADS):
            qh = jnp.concatenate([q[:, h * MLA_NOPE:(h + 1) * MLA_NOPE],
                                  qr_rot[:, h * MLA_ROPE:(h + 1) * MLA_ROPE], zpad], axis=-1)
            q_ref[h] = (qh * (MLA_SCALE * LOG2E)).astype(BF16)


def _mla_prep(p3, qnw, kvnw, wuq, wukv, cos_t, sin_t, *, with_q):
    b, n, _ = p3.shape
    tm = min(256, n)
    qk_spec = pl.BlockSpec((None, HEADS, tm, QK_PAD), lambda bi, i: (bi, 0, i, 0))
    v_spec = pl.BlockSpec((None, HEADS, tm, V_PAD), lambda bi, i: (bi, 0, i, 0))
    qk_shape = jax.ShapeDtypeStruct((b, HEADS, n, QK_PAD), BF16)
    v_shape = jax.ShapeDtypeStruct((b, HEADS, n, V_PAD), BF16)
    out_specs = [qk_spec, v_spec]
    out_shape = [qk_shape, v_shape]
    if with_q:
        out_specs = [qk_spec] + out_specs
        out_shape = [qk_shape] + out_shape
    full = lambda shape: pl.BlockSpec(shape, lambda bi, i: (0,) * len(shape))
    return pl.pallas_call(
        functools.partial(_mla_prep_kernel, with_q=with_q),
        grid=(b, n // tm),
        in_specs=[pl.BlockSpec((None, tm, 1024), lambda bi, i: (bi, i, COL_MISC)),
                  full((1, MLA_Q_RANK)), full((1, MLA_KV_RANK)),
                  full(wuq.shape), full(wukv.shape),
                  pl.BlockSpec((tm, HEADS * MLA_ROPE), lambda bi, i: (i, 0)),
                  pl.BlockSpec((tm, HEADS * MLA_ROPE), lambda bi, i: (i, 0))],
        out_specs=out_specs,
        out_shape=out_shape,
        compiler_params=_cparams("arbitrary", "arbitrary"),
        name="mla_prep_q" if with_q else "mla_prep_kv",
    )(p3, qnw, kvnw, wuq, wukv, cos_t, sin_t)


def _flash_kernel(q_ref, kc_ref, vc_ref, kl_ref, vl_ref, o_ref, s_scr, p_scr, al_scr, m_scr, acc_scr,
                  *, tk, n_chunks, n_sub):
    tq = q_ref.shape[0]
    sub = tq // n_sub
    qs = [q_ref[i * sub:(i + 1) * sub, :] for i in range(n_sub)]

    subs = range(n_sub)

    def scores_to(slot, k):
        for i in subs:
            s_scr[slot, i] = _dot_nt(qs[i], k)

    def softmax_to(s_slot, p_slot):
        ss = [s_scr[s_slot, i] for i in subs]
        m_old = [m_scr[i] for i in subs]
        m_new = [jnp.maximum(m, jnp.max(s, axis=-1, keepdims=True)) for s, m in zip(ss, m_old)]
        for i in subs:
            p_scr[p_slot, i] = jnp.exp2(ss[i] - m_new[i]).astype(BF16)
            al_scr[p_slot, i] = jnp.exp2(m_old[i] - m_new[i])
            m_scr[i] = m_new[i]

    def accumulate(p_slot, v):
        pv = [_dot(p_scr[p_slot, i], v) for i in subs]
        for i in subs:
            acc_scr[i] = al_scr[p_slot, i] * acc_scr[i] + pv[i]

    def chunk(ref, c):
        return ref[pl.ds(pl.multiple_of(c * tk, tk), tk), :]

    s_ctx = [_dot_nt(qs[i], kc_ref[...]) for i in subs]
    scores_to(0, chunk(kl_ref, 0))
    m_ctx = [jnp.max(s, axis=-1, keepdims=True) for s in s_ctx]
    p_ctx = [jnp.exp2(s - m).astype(BF16) for s, m in zip(s_ctx, m_ctx)]
    scores_to(1, chunk(kl_ref, 1))
    for i in subs:
        m_scr[i] = m_ctx[i]
        acc_scr[i] = _dot(p_ctx[i], vc_ref[...])
    softmax_to(0, 0)

    def body(j, _):
        c = 2 * j
        scores_to(0, chunk(kl_ref, c + 2))
        accumulate(0, chunk(vl_ref, c))
        softmax_to(1, 1)
        scores_to(1, chunk(kl_ref, c + 3))
        accumulate(1, chunk(vl_ref, c + 1))
        softmax_to(0, 0)
        return 0

    lax.fori_loop(0, (n_chunks - 2) // 2, body, 0)
    accumulate(0, chunk(vl_ref, n_chunks - 2))
    softmax_to(1, 1)
    accumulate(1, chunk(vl_ref, n_chunks - 1))
    for i in subs:
        acc = acc_scr[i]
        o_ref[i * sub:(i + 1) * sub, :] = (acc[:, :MLA_V] / acc[:, MLA_V:MLA_V + 1]).astype(o_ref.dtype)


def _flash(q, kc, vc, kl, vl):
    b, h, n, _ = q.shape
    nc = kc.shape[2]
    tq = min(1024, n)
    tk = min(512, n // 2)
    sub = min(256, tq)
    n_sub = tq // sub
    assert (n // tk) % 2 == 0
    return pl.pallas_call(
        functools.partial(_flash_kernel, tk=tk, n_chunks=n // tk, n_sub=n_sub),
        grid=(b, h, n // tq),
        in_specs=[pl.BlockSpec((None, None, tq, QK_PAD), lambda bi, hi, i: (bi, hi, i, 0)),
                  pl.BlockSpec((None, None, nc, QK_PAD), lambda bi, hi, i: (bi, hi, 0, 0)),
                  pl.BlockSpec((None, None, nc, V_PAD), lambda bi, hi, i: (bi, hi, 0, 0)),
                  pl.BlockSpec((None, None, n, QK_PAD), lambda bi, hi, i: (bi, hi, 0, 0)),
                  pl.BlockSpec((None, None, n, V_PAD), lambda bi, hi, i: (bi, hi, 0, 0))],
        out_specs=pl.BlockSpec((None, tq, MLA_V), lambda bi, hi, i: (bi, i, hi)),
        out_shape=jax.ShapeDtypeStruct((b, n, h * MLA_V), BF16),
        scratch_shapes=[pltpu.VMEM((2, n_sub, sub, tk), F32), pltpu.VMEM((2, n_sub, sub, tk), BF16),
                        pltpu.VMEM((2, n_sub, sub, 1), F32), pltpu.VMEM((n_sub, sub, 1), F32),
                        pltpu.VMEM((n_sub, sub, V_PAD), F32)],
        compiler_params=_cparams("arbitrary", "arbitrary", "arbitrary"),
        name="flash",
    )(q, kc, vc, kl, vl)


def _dn_conv_kernel(cur_ref, prev_ref, next_ref, cw_ref, gl_ref, alog_ref, dtb_ref, o_ref, g_ref, scr, *, n_tiles):
    i = pl.program_id(1)
    part = pl.program_id(2)
    tm = cur_ref.shape[0]
    scr[pl.ds(0, 8), :] = jnp.where(i > 0, prev_ref[...], 0.0)
    scr[pl.ds(8, tm), :] = cur_ref[...]
    scr[pl.ds(8 + tm, 8), :] = jnp.where(i < n_tiles - 1, next_ref[...], 0.0)
    cw = cw_ref[...]
    y = scr[pl.ds(8 - CONV_W // 2, tm), :] * cw[0:1, :]
    for w in range(1, CONV_W):
        y = y + scr[pl.ds(8 - CONV_W // 2 + w, tm), :] * cw[w:w + 1, :]
    y = _silu(y)

    @pl.when(part == 2)
    def _():
        o_ref[...] = y

    @pl.when(part < 2)
    def _():
        scale = jnp.where(part == 0, DN_DK ** -0.5, 1.0).astype(F32)
        for h in range(HEADS):
            yh = y[:, h * DN_DK:(h + 1) * DN_DK]
            yh = yh * lax.rsqrt(jnp.sum(yh * yh, axis=-1, keepdims=True) + EPS)
            o_ref[:, h * DN_DK:(h + 1) * DN_DK] = yh * scale

    @pl.when(part == 0)
    def _():
        val = gl_ref[...]
        lane = lax.broadcasted_iota(jnp.int32, val.shape, 1)
        g = -jnp.exp(alog_ref[...]) * jax.nn.softplus(val + dtb_ref[...])
        g_ref[...] = jnp.where(lane < 2 * HEADS, g, jax.nn.sigmoid(val))


def _dn_conv(p3, conv_w, alog_row, dtb_row):
    b, n, _ = p3.shape
    tm = min(256, n)
    nt = n // tm
    r8 = tm // 8
    return pl.pallas_call(
        functools.partial(_dn_conv_kernel, n_tiles=nt),
        grid=(b, nt, 3),
        in_specs=[pl.BlockSpec((None, tm, 1024), lambda bi, i, p: (bi, i, COL_QKV + p)),
                  pl.BlockSpec((None, 8, 1024), lambda bi, i, p: (bi, jnp.maximum(i * r8 - 1, 0), COL_QKV + p)),
                  pl.BlockSpec((None, 8, 1024), lambda bi, i, p: (bi, jnp.minimum((i + 1) * r8, nt * r8 - 1), COL_QKV + p)),
                  pl.BlockSpec((CONV_W, 1024), lambda bi, i, p: (0, p)),
                  pl.BlockSpec((None, tm, 128), lambda bi, i, p: (bi, i, MISC_GATES // 128)),
                  pl.BlockSpec((1, 128), lambda bi, i, p: (0, 0)),
                  pl.BlockSpec((1, 128), lambda bi, i, p: (0, 0))],
        out_specs=[pl.BlockSpec((None, tm, 1024), lambda bi, i, p: (bi, i, p)),
                   pl.BlockSpec((None, tm, 128), lambda bi, i, p: (bi, i, 0))],
        out_shape=[jax.ShapeDtypeStruct((b, n, DN_QKV), F32),
                   jax.ShapeDtypeStruct((b, n, 128), F32)],
        scratch_shapes=[pltpu.VMEM((tm + 16, 1024), F32)],
        compiler_params=_cparams("arbitrary", "arbitrary", "arbitrary"),
        name="dn_conv",
    )(p3, p3, p3, conv_w, p3, alog_row, dtb_row)


def _dn_chunk_kernel(q_ref, k_ref, v_ref, g_ref, *out_refs):
    ri = lax.broadcasted_iota(jnp.int32, (CHUNK, CHUNK), 0)
    ci = lax.broadcasted_iota(jnp.int32, (CHUNK, CHUNK), 1)
    eye = (ri == ci).astype(F32)
    incl = (ri >= ci, ri <= ci)
    strict = (ri > ci, ri < ci)
    gates = g_ref[...]
    gc = [jnp.dot(incl[d].astype(F32), gates, precision=HIGHEST, preferred_element_type=F32) for d in range(2)]
    gct = [g.T for g in gc]
    chains = [(h, d) for h in range(HEADS) for d in range(2)]
    col_of = lambda h: slice(h * DN_DK, (h + 1) * DN_DK)
    q = [q_ref[:, col_of(h)] for h in range(HEADS)]
    k = [k_ref[:, col_of(h)] for h in range(HEADS)]
    k16 = [kh.astype(BF16) for kh in k]
    kk = [_dot_nt(k16[h], k16[h]) for h in range(HEADS)]
    qk = [_dot_nt(q[h].astype(BF16), k16[h]) for h in range(HEADS)]
    gcol, glast, beta, decay, t, pw = {}, {}, {}, {}, {}, {}
    for h, d in chains:
        c = (h, d)
        col = d * HEADS + h
        last = CHUNK - 1 if d == 0 else 0
        gcol[c] = gc[d][:, col:col + 1]
        glast[c] = gc[d][last:last + 1, col:col + 1]
        beta[c] = gates[:, 2 * HEADS + col:2 * HEADS + col + 1]
        decay[c] = jnp.exp(jnp.where(incl[d], gcol[c] - gct[d][col:col + 1, :], -jnp.inf))
        pw[c] = jnp.where(strict[d], -(beta[c] * kk[h] * decay[c]), 0.0)
        t[c] = eye + pw[c]
    for _ in range(5):
        for c in chains:
            pw16 = pw[c].astype(BF16)
            pw[c] = _dot(pw16, pw16)
        for c in chains:
            t[c] = t[c] + _dot(t[c].astype(BF16), pw[c].astype(BF16))
    sol = {}
    for h, d in chains:
        c = (h, d)
        vh = v_ref[:, col_of(h)]
        rhs = jnp.concatenate([vh * beta[c], k[h] * beta[c] * jnp.exp(gcol[c])], axis=-1).astype(BF16)
        sol[c] = _dot(t[c].astype(BF16), rhs)
    for h, d in chains:
        c = (h, d)
        cs = col_of(h)
        u_ref, w_ref, qg_ref, kd_ref, att_ref, egl_ref = out_refs[6 * d:6 * d + 6]
        u_ref[:, cs] = sol[c][:, :DN_DV]
        w_ref[:, cs] = sol[c][:, DN_DV:].astype(BF16)
        qg_ref[:, cs] = (q[h] * jnp.exp(gcol[c])).astype(BF16)
        kd_ref[:, cs] = (k[h] * jnp.exp(glast[c] - gcol[c])).astype(BF16)
        att_ref[:, h * CHUNK:(h + 1) * CHUNK] = jnp.where(incl[d], qk[h] * decay[c], 0.0).astype(BF16)
        egl_ref[h:h + 1, :] = jnp.broadcast_to(jnp.exp(glast[c]), (1, 128))


def _dn_chunk(qkvn, gates):
    b, n, _ = qkvn.shape
    nchunks = n // CHUNK
    tm = CHUNK
    hd = HEADS * DN_DK
    row = lambda w: pl.BlockSpec((None, tm, w), lambda bi, i: (bi, i, 0))
    one_dir_specs = [row(hd), row(hd), row(hd), row(hd), row(HEADS * CHUNK),
                     pl.BlockSpec((None, HEADS, 128), lambda bi, i: (bi, i, 0))]
    one_dir_shapes = [jax.ShapeDtypeStruct((b, n, hd), F32),
                      jax.ShapeDtypeStruct((b, n, hd), BF16),
                      jax.ShapeDtypeStruct((b, n, hd), BF16),
                      jax.ShapeDtypeStruct((b, n, hd), BF16),
                      jax.ShapeDtypeStruct((b, n, HEADS * CHUNK), BF16),
                      jax.ShapeDtypeStruct((b, nchunks * HEADS, 128), F32)]
    outs = pl.pallas_call(
        _dn_chunk_kernel,
        grid=(b, nchunks),
        in_specs=[pl.BlockSpec((None, tm, hd), lambda bi, i: (bi, i, 0)),
                  pl.BlockSpec((None, tm, hd), lambda bi, i: (bi, i, 1)),
                  pl.BlockSpec((None, tm, hd), lambda bi, i: (bi, i, 2)),
                  pl.BlockSpec((None, tm, 128), lambda bi, i: (bi, i, 0))],
        out_specs=one_dir_specs * 2,
        out_shape=one_dir_shapes * 2,
        compiler_params=_cparams("arbitrary", "arbitrary"),
        name="dn_chunk",
    )(qkvn, qkvn, qkvn, gates)
    return outs[:6], outs[6:]


def _dn_scan_kernel(s0_ref, *refs, n_batch):
    ins = refs[:12]
    of_ref, ob_ref, sf_ref, s_scr = refs[12:]
    step = pl.program_id(0)

    @pl.when(step == 0)
    def _():
        s_scr[...] = s0_ref[...]

    chains = [(d, b, h) for d in range(2) for b in range(n_batch) for h in range(HEADS)]
    col_of = lambda h: slice(h * DN_DK, (h + 1) * DN_DK)
    s16, ws, qs, v16 = {}, {}, {}, {}
    for c in chains:
        s16[c] = s_scr[c].astype(BF16)
    for c in chains:
        d, b, h = c
        ws[c] = _dot(ins[6 * d + 1][b, :, col_of(h)], s16[c])
        qs[c] = _dot(ins[6 * d + 2][b, :, col_of(h)], s16[c])
    for c in chains:
        d, b, h = c
        v16[c] = (ins[6 * d][b, :, col_of(h)] - ws[c]).astype(BF16)
    for c in chains:
        d, b, h = c
        att = ins[6 * d + 4][b, :, h * CHUNK:(h + 1) * CHUNK]
        o_ref = of_ref if d == 0 else ob_ref
        o_ref[b, :, col_of(h)] = qs[c] + _dot(att, v16[c])
        s_scr[c] = s_scr[c] * ins[6 * d + 5][b, h:h + 1, :] + _dot_tn(ins[6 * d + 3][b, :, col_of(h)], v16[c])

    @pl.when(step == pl.num_programs(0) - 1)
    def _():
        sf_ref[...] = s_scr[...]


def _dn_scan(s0, fwd, bwd):
    b, n, hd = fwd[0].shape
    nchunks = n // CHUNK
    f_idx = lambda s: (0, s, 0)
    b_idx = lambda s: (0, nchunks - 1 - s, 0)

    def specs(idx):
        return [pl.BlockSpec((b, CHUNK, hd), idx)] * 4 + [pl.BlockSpec((b, CHUNK, HEADS * CHUNK), idx),
                                                          pl.BlockSpec((b, HEADS, 128), idx)]

    state_spec = pl.BlockSpec(s0.shape, lambda s: (0,) * 5)
    o_f, o_b, s_fin = pl.pallas_call(
        functools.partial(_dn_scan_kernel, n_batch=b),
        grid=(nchunks,),
        in_specs=[state_spec] + specs(f_idx) + specs(b_idx),
        out_specs=[pl.BlockSpec((b, CHUNK, hd), f_idx), pl.BlockSpec((b, CHUNK, hd), b_idx), state_spec],
        out_shape=[jax.ShapeDtypeStruct((b, n, hd), F32), jax.ShapeDtypeStruct((b, n, hd), F32),
                   jax.ShapeDtypeStruct(s0.shape, F32)],
        scratch_shapes=[pltpu.VMEM(s0.shape, F32)],
        compiler_params=_cparams("arbitrary"),
        name="dn_scan",
    )(s0, *fwd, *bwd)
    return o_f, o_b, s_fin


def _merge_kernel(ya_ref, of_ref, ob_ref, z_ref, ga_ref, gb_ref, x_ref, g1_ref, sh2_ref, sc2_ref,
                  dnw_ref, n2w_ref, wa_ref, wb_ref, wo_ref, x1_ref, h2_ref):
    b = pl.program_id(0)
    o = of_ref[...] + ob_ref[...]
    z = z_ref[...]
    dnw = dnw_ref[...]
    parts = []
    for h in range(HEADS):
        cs = slice(h * DN_DV, (h + 1) * DN_DV)
        parts.append(_rms(o[:, cs]) * dnw * _silu(z[:, cs]))
    yb = jnp.concatenate(parts, axis=-1).astype(BF16)
    t = (jax.nn.sigmoid(ga_ref[...]) * _dot(ya_ref[...], wa_ref[...])
         + jax.nn.sigmoid(gb_ref[...]) * _dot(yb, wb_ref[...]))
    y = _dot(t.astype(BF16), wo_ref[...])
    x1 = x_ref[...] + g1_ref[pl.ds(b, 1), :] * y
    x1_ref[...] = x1
    h2 = _rms(x1) * n2w_ref[...] * (1.0 + sc2_ref[pl.ds(b, 1), :]) + sh2_ref[pl.ds(b, 1), :]
    h2_ref[...] = h2.astype(BF16)


def _merge(ya, o_f, o_b, p3, x, mod, dn_norm_w, norm2_w, wa, wb, wo):
    b, n, _ = x.shape
    tm = min(256, n)
    row = lambda w, j=0: pl.BlockSpec((None, tm, w), lambda bi, i: (bi, i, j))
    modrow = lambda j: pl.BlockSpec((8, D_MODEL), lambda bi, i: (0, j))
    const = lambda shape: pl.BlockSpec(shape, lambda bi, i: (0,) * len(shape), pipeline_mode=pl.Buffered(1))
    return pl.pallas_call(
        _merge_kernel,
        grid=(b, n // tm),
        in_specs=[row(1024), row(1024), row(1024), row(1024, COL_Z), row(D_MODEL, COL_GA), row(D_MODEL, COL_GB),
                  row(D_MODEL), modrow(2), modrow(3), modrow(4),
                  const((1, DN_DV)), const((1, D_MODEL)),
                  const(wa.shape), const(wb.shape), const(wo.shape)],
        out_specs=[row(D_MODEL), row(D_MODEL)],
        out_shape=[jax.ShapeDtypeStruct((b, n, D_MODEL), F32), jax.ShapeDtypeStruct((b, n, D_MODEL), BF16)],
        compiler_params=_cparams("arbitrary", "arbitrary"),
        name="merge",
    )(ya, o_f, o_b, p3, p3, p3, x, mod, mod, mod, dn_norm_w, norm2_w, wa, wb, wo)


def _top16_rows(w, out_scr):
    for n in range(PEER_TOPK):
        m = jnp.max(w, axis=0, keepdims=True)
        out_scr[n:n + 1, :] = m
        w = jnp.where(w == m, -jnp.inf, w)


_PAIR_COUNTS = tuple(PEER_TOPK // (m + 1) for m in range(PEER_TOPK))
_N_CAND = sum(_PAIR_COUNTS)
_CAND_ROWS = -(-_N_CAND // 8) * 8


def _peer_topk_kernel(h2_ref, wq_ref, sk_ref, cut_ref, s1_ref, e0_ref, e1_ref, q_scr, a_scr, b_scr, c_scr):
    q_scr[...] = _dot(h2_ref[...], wq_ref[...])
    tn = q_scr.shape[0]

    def head_body(h, _):
        off0 = pl.multiple_of(h * 2 * PEER_KEYS, PEER_KEYS)
        off1 = pl.multiple_of(h * 2 * PEER_KEYS + PEER_KEYS, PEER_KEYS)
        s0 = _dot_nt(sk_ref[0], q_scr[:, pl.ds(off0, PEER_KEYS)], precision=HIGHEST)
        s1 = _dot_nt(sk_ref[1], q_scr[:, pl.ds(off1, PEER_KEYS)], precision=HIGHEST)
        _top16_rows(s0, a_scr)
        _top16_rows(s1, b_scr)
        btop = b_scr[...]
        c_scr[_CAND_ROWS - 8:, :] = jnp.full((8, tn), -jnp.inf, F32)
        off = 0
        for m, cnt in enumerate(_PAIR_COUNTS):
            c_scr[off:off + cnt, :] = a_scr[m:m + 1, :] + btop[:cnt, :]
            off += cnt
        cand = c_scr[...]
        zsum = jnp.zeros((1, tn), F32)
        for n in range(PEER_TOPK):
            best = jnp.max(cand, axis=0, keepdims=True)
            if n == 0:
                top = best
            zsum = zsum + jnp.exp(best - top)
            cand = jnp.where(cand == best, -jnp.inf, cand)
        tau = best
        cut = jnp.full(s0.shape, jnp.inf, F32)
        for n in range(PEER_TOPK):
            bn = btop[n:n + 1, :]
            cut = jnp.minimum(cut, jnp.where(s0 + bn >= tau, bn, jnp.inf))
        cut_ref[h] = cut
        s1_ref[h] = s1
        e0_ref[h] = jnp.exp(s0 - a_scr[0:1, :]) / zsum
        e1_ref[h] = jnp.exp(s1 - btop[0:1, :])
        return 0

    lax.fori_loop(0, HEADS, head_body, 0)


def _peer_topk(h2, wq, sub_keys):
    t = h2.shape[0]
    tn = min(256, t)
    big = pl.BlockSpec((HEADS, PEER_KEYS, tn), lambda i: (0, 0, i))
    big_shape = jax.ShapeDtypeStruct((HEADS, PEER_KEYS, t), F32)
    return pl.pallas_call(
        _peer_topk_kernel,
        grid=(t // tn,),
        in_specs=[pl.BlockSpec((tn, D_MODEL), lambda i: (i, 0)),
                  pl.BlockSpec(wq.shape, lambda i: (0, 0), pipeline_mode=pl.Buffered(1)),
                  pl.BlockSpec(sub_keys.shape, lambda i: (0, 0, 0))],
        out_specs=[big, big, big, big],
        out_shape=[big_shape, big_shape, big_shape, big_shape],
        scratch_shapes=[pltpu.VMEM((tn, wq.shape[1]), F32), pltpu.VMEM((PEER_TOPK, tn), F32),
                        pltpu.VMEM((PEER_TOPK, tn), F32), pltpu.VMEM((_CAND_ROWS, tn), F32)],
        compiler_params=_cparams("arbitrary"),
        name="peer_topk",
    )(h2, wq, sub_keys)


def _peer_dense_kernel(h2_ref, u_ref, vt_ref, cut_ref, s1_ref, e0_ref, e1_ref, x1_ref, g2_ref, fw_ref,
                       o_ref, acc, g_scr, *, tiles_per_batch):
    e = pl.program_id(1)
    eb = u_ref.shape[0]
    tn = h2_ref.shape[0]
    piece = 2 * PEER_KEYS
    jh = PEER_KEYS // 2

    @pl.when(e == 0)
    def _():
        acc[...] = jnp.zeros_like(acc)

    def route_and_gate(p, a):
        for lg in range(tn // 128):
            ls = slice(lg * 128, (lg + 1) * 128)
            for half in range(2):
                js = slice(half * jh, (half + 1) * jh)
                tots = [jnp.zeros((jh, 128), F32), jnp.zeros((jh, 128), F32)]
                for h in range(HEADS):
                    s1h = s1_ref[h, js, ls]
                    e1h = e1_ref[h, js, ls]
                    for ii in range(2):
                        il = 2 * p + ii
                        sel = s1h >= cut_ref[h, il:il + 1, ls]
                        tots[ii] = tots[ii] + jnp.where(sel, e1h, 0.0) * e0_ref[h, il:il + 1, ls]
                for ii in range(2):
                    r0 = ii * PEER_KEYS + half * jh
                    ap = a[r0:r0 + jh, ls]
                    gated = 0.5 * ap * (1.0 + lax.erf(ap * (2.0 ** -0.5))) * tots[ii]
                    g_scr[p * piece + r0:p * piece + r0 + jh, ls] = gated.astype(BF16)

    prev = None
    for p in range(eb // piece):
        a = _dot_nt(u_ref[p * piece:(p + 1) * piece, :], h2_ref[...])
        if prev is not None:
            route_and_gate(p - 1, prev)
        prev = a
    route_and_gate(eb // piece - 1, prev)
    acc[...] += _dot(vt_ref[...], g_scr[...])

    @pl.when(e == pl.num_programs(1) - 1)
    def _():
        b = pl.program_id(0) // tiles_per_batch
        xo = x1_ref[...] + g2_ref[pl.ds(b, 1), :] * acc[...].T
        o_ref[...] = _rms(xo) * fw_ref[...]


def _peer_dense(h2, u16, vt16, cut, s1, e0, e1, x1, mod, final_w, *, rows_per_batch):
    t = h2.shape[0]
    tn = min(512, rows_per_batch)
    eb = 8 * PEER_KEYS
    big = pl.BlockSpec((HEADS, PEER_KEYS, tn), lambda i, e: (0, 0, i))
    rows = pl.BlockSpec((HEADS, eb // PEER_KEYS, tn), lambda i, e: (0, e, i))
    return pl.pallas_call(
        functools.partial(_peer_dense_kernel, tiles_per_batch=rows_per_batch // tn),
        grid=(t // tn, PEER_EXPERTS // eb),
        in_specs=[pl.BlockSpec((tn, D_MODEL), lambda i, e: (i, 0), pipeline_mode=pl.Buffered(1)),
                  pl.BlockSpec((eb, D_MODEL), lambda i, e: (e, 0)),
                  pl.BlockSpec((D_MODEL, eb), lambda i, e: (0, e)),
                  rows, big, rows, big,
                  pl.BlockSpec((tn, D_MODEL), lambda i, e: (i, 0), pipeline_mode=pl.Buffered(1)),
                  pl.BlockSpec((8, D_MODEL), lambda i, e: (0, 5)),
                  pl.BlockSpec((1, D_MODEL), lambda i, e: (0, 0))],
        out_specs=pl.BlockSpec((tn, D_MODEL), lambda i, e: (i, 0)),
        out_shape=jax.ShapeDtypeStruct((t, D_MODEL), F32),
        scratch_shapes=[pltpu.VMEM((D_MODEL, tn), F32), pltpu.VMEM((eb, tn), BF16)],
        compiler_params=_cparams("arbitrary", "arbitrary"),
        name="peer_dense",
    )(h2, u16, vt16, cut, s1, e0, e1, x1, mod, final_w)


def _swap_rot_halves(w, n_heads):
    lead = w.shape[0]
    return w.reshape(lead, n_heads, 2, 2, MLA_ROPE // 4)[:, :, :, ::-1, :].reshape(lead, n_heads * MLA_ROPE)


def _permute_w_in(w_in):
    o_qkv = MLA_Q_RANK + MLA_KV_RANK + MLA_ROPE
    o_z = o_qkv + DN_QKV
    o_g = o_z + HEADS * DN_DV
    o_ga = o_g + 4 * HEADS
    kr = w_in[:, MLA_Q_RANK + MLA_KV_RANK:o_qkv]
    misc_pad = jnp.zeros((w_in.shape[0], 1024 - MISC_GATES - 4 * HEADS), w_in.dtype)
    return jnp.concatenate([w_in[:, :o_qkv], _swap_rot_halves(kr, 1), w_in[:, o_g:o_ga], misc_pad,
                            w_in[:, o_z:o_g], w_in[:, o_ga:], w_in[:, o_qkv:o_z]], axis=1).astype(BF16)


def _rope_tables(n, n_heads):
    rows = (jnp.arange(n) // GRID_W).astype(F32)
    cols = (jnp.arange(n) % GRID_W).astype(F32)
    axis_dim = MLA_ROPE // 2
    inv_freq = ROPE_BASE ** (-jnp.arange(0, axis_dim, 2, dtype=F32) / axis_dim)
    ar = rows[:, None] * inv_freq
    ac = cols[:, None] * inv_freq
    cos = jnp.concatenate([jnp.cos(ar), jnp.cos(ar), jnp.cos(ac), jnp.cos(ac)], axis=-1)
    sin = jnp.concatenate([-jnp.sin(ar), jnp.sin(ar), -jnp.sin(ac), jnp.sin(ac)], axis=-1)
    return jnp.tile(cos, (1, n_heads)), jnp.tile(sin, (1, n_heads))


def _mixer_stage(x, ctx, mod, norm1_w, w_in, q_norm_w, kv_norm_w, w_uq, w_ukv, conv_w, a_log, dt_bias, dn_norm_w,
                 w_branch_a, w_branch_b, w_out, norm2_w):
    b, n, d = x.shape
    nctx = ctx.shape[1]

    w_perm = _permute_w_in(w_in)
    uq = w_uq.reshape(MLA_Q_RANK, HEADS, MLA_NOPE + MLA_ROPE)
    uq_rope = uq[:, :, MLA_NOPE:].reshape(MLA_Q_RANK, HEADS * MLA_ROPE)
    wuq = jnp.concatenate([uq[:, :, :MLA_NOPE].reshape(MLA_Q_RANK, HEADS * MLA_NOPE), uq_rope,
                           _swap_rot_halves(uq_rope, HEADS)], axis=1).astype(BF16)
    ukv = w_ukv.reshape(MLA_KV_RANK, HEADS, MLA_NOPE + MLA_V)
    wukv = jnp.concatenate([ukv[:, :, :MLA_NOPE].reshape(MLA_KV_RANK, HEADS * MLA_NOPE),
                            ukv[:, :, MLA_NOPE:].reshape(MLA_KV_RANK, HEADS * MLA_V)], axis=1).astype(BF16)
    pad = jnp.zeros((128 - 2 * HEADS,), F32)
    alog_row = jnp.concatenate([a_log.reshape(-1), pad])[None, :]
    dtb_row = jnp.concatenate([dt_bias.reshape(-1), pad])[None, :]

    p_l = _in_proj(x.reshape(b * n, d), mod, norm1_w[None, :], w_perm, rows_per_batch=n, row0=0).reshape(b, n, D_IN_PAD)
    p_c = _in_proj(ctx.reshape(b * nctx, d), mod, norm1_w[None, :], w_perm,
                   rows_per_batch=b * nctx, row0=b).reshape(b, nctx, D_IN_PAD)

    cos_l, sin_l = _rope_tables(n, HEADS)
    cos_c = jnp.ones((nctx, HEADS * MLA_ROPE), F32)
    sin_c = jnp.zeros((nctx, HEADS * MLA_ROPE), F32)
    qnw, kvnw = q_norm_w[None, :], kv_norm_w[None, :]
    q_l, k_l, v_l = _mla_prep(p_l, qnw, kvnw, wuq, wukv, cos_l, sin_l, with_q=True)
    k_c, v_c = _mla_prep(p_c, qnw, kvnw, wuq, wukv, cos_c, sin_c, with_q=False)
    y_a = _flash(q_l, k_c, v_c, k_l, v_l)

    qkv_c, gates_c = _dn_conv(p_c, conv_w, alog_row, dtb_row)
    qkv_l, gates_l = _dn_conv(p_l, conv_w, alog_row, dtb_row)
    s_zero = jnp.zeros((2, b, HEADS, DN_DK, DN_DV), F32)
    _, _, s_ctx = _dn_scan(s_zero, *_dn_chunk(qkv_c, gates_c))
    o_f, o_b, _ = _dn_scan(s_ctx, *_dn_chunk(qkv_l, gates_l))

    return _merge(y_a, o_f, o_b, p_l, x, mod, dn_norm_w[None, :], norm2_w[None, :],
                  w_branch_a.astype(BF16), w_branch_b.astype(BF16), w_out.astype(BF16))


def _peer_stage(x1, h2, mod, peer_w_q, peer_sub_keys, peer_u, peer_v, final_norm_w):
    b, n, d = x1.shape
    h2f = h2.reshape(b * n, d)
    cut, s1, e0, e1 = _peer_topk(h2f, peer_w_q.astype(BF16), peer_sub_keys)
    out = _peer_dense(h2f, peer_u.astype(BF16), peer_v.T.astype(BF16), cut, s1, e0, e1,
                      x1.reshape(b * n, d), mod, final_norm_w[None, :], rows_per_batch=n)
    return out.reshape(b, n, d)


def kernel(x, c, ctx, c_ctx, w_mod, b_mod, norm1_w, w_in, mla_q_norm_w, mla_kv_norm_w, w_uq, w_ukv, dn_conv_w, dn_a_log, dn_dt_bias, dn_norm_w, w_branch_a, w_branch_b, w_out, norm2_w, peer_w_q, peer_sub_keys, peer_u, peer_v, final_norm_w):
    depth = w_mod.shape[0]
    assert depth == 1, "single-layer block: the context stream is never updated"
    b = x.shape[0]
    cmat = jnp.concatenate([c, c_ctx[None, :], jnp.zeros((8 - b - 1, c.shape[1]), F32)], axis=0)
    mod = _mod(cmat, w_mod[0], b_mod[0][None, :])
    x1, h2 = _mixer_stage(x, ctx, mod, norm1_w[0], w_in[0], mla_q_norm_w[0], mla_kv_norm_w[0], w_uq[0], w_ukv[0],
                          dn_conv_w[0], dn_a_log[0], dn_dt_bias[0], dn_norm_w[0], w_branch_a[0], w_branch_b[0],
                          w_out[0], norm2_w[0])
    return _peer_stage(x1, h2, mod, peer_w_q[0], peer_sub_keys[0], peer_u[0], peer_v[0], final_norm_w)
```
